```python
import jax, jax.numpy as jnp
from jax import lax
import numpy as np

D_MODEL = 1024
BATCH = 16
SEQ = 4096
DEPTH = 4

CHUNK = 64
RWKV_HEADS = 8
RWKV_HEAD_DIM = 64
RWKV_WIDTH = RWKV_HEADS * RWKV_HEAD_DIM
DECAY_LORA = 64
AAA_LORA = 64
VRES_LORA = 32
GATE_LORA = 128
GMLP_BLOCK = 128
GMLP_GROUPS = 4
GMLP_WIDTH = 512
GMLP_GROUP_DIM = GMLP_WIDTH // GMLP_GROUPS
D_FF = 4 * D_MODEL
N_MOD = 6
RMS_EPS = 1e-6
LN_EPS = 1e-5
GN_EPS = 64e-5

RWKV_COLS = 3 * RWKV_WIDTH + DECAY_LORA + AAA_LORA + GATE_LORA
GMLP_COLS = 2 * GMLP_WIDTH
GATE_COLS = 2 * D_MODEL
IN_COLS = RWKV_COLS + GMLP_COLS + GATE_COLS
RWKV_SPLITS = (RWKV_WIDTH, 2 * RWKV_WIDTH, 3 * RWKV_WIDTH,
               3 * RWKV_WIDTH + DECAY_LORA, 3 * RWKV_WIDTH + DECAY_LORA + AAA_LORA)

kernel_name = 'hybrid_rwkv7_gmlp_adaln_trunk'


def rms_norm(x, gain):
    xf = x.astype(jnp.float32)
    ms = jnp.mean(xf * xf, axis=-1, keepdims=True)
    return (xf * lax.rsqrt(ms + RMS_EPS)).astype(x.dtype) * gain


def modulate(h, shift, scale):
    return h * (1.0 + scale[:, None, :]) + shift[:, None, :]


def token_shift(p):
    return jnp.pad(p[:, :-1], ((0, 0), (1, 0), (0, 0)))


def wkv7_scan(r, decay, k, v, a, b):
    def step(state, inp):
        r_t, w_t, k_t, v_t, a_t, b_t = inp
        sa = jnp.einsum('bhij,bhj->bhi', state, a_t)
        state = (state * w_t[:, :, None, :] + sa[..., None] * b_t[:, :, None, :]
                 + v_t[..., None] * k_t[:, :, None, :])
        return state, jnp.einsum('bhij,bhj->bhi', state, r_t)
    xs = tuple(jnp.moveaxis(t.astype(jnp.float32), 1, 0) for t in (r, decay, k, v, a, b))
    B, S, H, N = r.shape
    s0 = jnp.zeros((B, H, N, N), jnp.float32)
    _, ys = lax.scan(step, s0, xs)
    return jnp.moveaxis(ys, 0, 1)


def rwkv7_mix(p_rw, mu, w0, w2_w, a0, w2_a, w2_g, k_k, k_a, r_k, gn_w, gn_b, v_first, vres):
    B, S, _ = p_rw.shape
    H, N = RWKV_HEADS, RWKV_HEAD_DIM
    xs = p_rw + (token_shift(p_rw) - p_rw) * mu
    r, k, v, w_lo, a_lo, g_lo = jnp.split(xs, RWKV_SPLITS, axis=-1)
    w_log = -jax.nn.softplus(-(w0 + jnp.tanh(w_lo) @ w2_w)) - 0.5
    decay = jnp.exp(-jnp.exp(w_log.astype(jnp.float32)))
    a = jax.nn.sigmoid(a0 + a_lo @ w2_a)
    g = jax.nn.sigmoid(g_lo) @ w2_g
    if vres is not None:
        v0, v1, v2 = vres
        v = v + (v_first - v) * jax.nn.sigmoid(v0 + (v @ v1) @ v2)
    kk = (k * k_k).reshape(B, S, H, N).astype(jnp.float32)
    kk = kk / jnp.maximum(jnp.sqrt(jnp.sum(kk * kk, axis=-1, keepdims=True)), 1e-12)
    k = k * (1.0 + (a - 1.0) * k_a)
    rh = r.reshape(B, S, H, N)
    kh = k.reshape(B, S, H, N)
    vh = v.reshape(B, S, H, N)
    ah = a.reshape(B, S, H, N)
    y = wkv7_scan(rh, decay.reshape(B, S, H, N), kh, vh, -kk, kk * ah)
    mean = jnp.mean(y, axis=-1, keepdims=True)
    var = jnp.mean(jnp.square(y - mean), axis=-1, keepdims=True)
    y = ((y - mean) * lax.rsqrt(var + GN_EPS)).reshape(B, S, RWKV_WIDTH) * gn_w + gn_b
    bonus = (jnp.sum(rh * kh * r_k, axis=-1, keepdims=True) * vh).reshape(B, S, RWKV_WIDTH)
    return (y + bonus) * g, v


def gmlp_mix(p_gm, ln_w, ln_b, w_s, b_s, mask):
    z = jax.nn.gelu(p_gm, approximate=False)
    u, v = jnp.split(z, 2, axis=-1)
    vf = v.astype(jnp.float32)
    mean = jnp.mean(vf, axis=-1, keepdims=True)
    var = jnp.mean(jnp.square(vf - mean), axis=-1, keepdims=True)
    v = ((vf - mean) * lax.rsqrt(var + LN_EPS)).astype(v.dtype) * ln_w + ln_b
    B, S, _ = v.shape
    vb = v.reshape(B, S // GMLP_BLOCK, GMLP_BLOCK, GMLP_GROUPS, GMLP_GROUP_DIM)
    ws = w_s * mask
    sv = jnp.einsum('gij,bnjgd->bnigd', ws, vb) + b_s.T[None, None, :, :, None]
    return u * sv.reshape(B, S, GMLP_WIDTH)


def setup_inputs(seed: int = 0) -> dict:
    key = jax.random.key(seed)
    ks = iter(jax.random.split(key, 40))
    nrm = lambda shape, s: jax.random.normal(next(ks), shape, jnp.float32) * s
    uni = lambda shape, lo, hi: jax.random.uniform(next(ks), shape, jnp.float32, lo, hi)
    L, D, RW, GW, T, G = DEPTH, D_MODEL, RWKV_WIDTH, GMLP_WIDTH, GMLP_BLOCK, GMLP_GROUPS
    return {
        'x': nrm((BATCH, SEQ, D), 1.0),
        'c': nrm((BATCH, D), 1.0),
        'w_ada': nrm((L, D, N_MOD * D), 0.2 * D ** -0.5),
        'b_ada': nrm((L, N_MOD * D), 0.02),
        'norm1_g': 1.0 + nrm((L, D), 0.05),
        'norm2_g': 1.0 + nrm((L, D), 0.05),
        'w_in': nrm((L, D, IN_COLS), D ** -0.5),
        'mu_shift': uni((L, RWKV_COLS), 0.0, 1.0),
        'w0_decay': uni((L, RW), -6.0, -1.0),
        'w2_decay': nrm((L, DECAY_LORA, RW), 0.1),
        'a0': nrm((L, RW), 0.1),
        'w2_aaa': nrm((L, AAA_LORA, RW), AAA_LORA ** -0.5),
        'w2_gate': nrm((L, GATE_LORA, RW), GATE_LORA ** -0.5),
        'k_k': 0.85 + nrm((L, RW), 0.05),
        'k_a': 1.0 + nrm((L, RW), 0.05),
        'r_k': nrm((L, RWKV_HEADS, RWKV_HEAD_DIM), 0.1),
        'gn_w': 1.0 + nrm((L, RW), 0.05),
        'gn_b': nrm((L, RW), 0.02),
        'v0_res': nrm((L - 1, RW), 0.1),
        'w1_res': nrm((L - 1, RW, VRES_LORA), RW ** -0.5),
        'w2_res': nrm((L - 1, VRES_LORA, RW), VRES_LORA ** -0.5),
        'ln_gmlp_w': 1.0 + nrm((L, GW), 0.05),
        'ln_gmlp_b': nrm((L, GW), 0.02),
        'w_spatial': nrm((L, G, T, T), 0.5 * T ** -0.5),
        'b_spatial': 1.0 + nrm((L, G, T), 0.1),
        'w_br_rwkv': nrm((L, RW, D), RW ** -0.5),
        'w_br_gmlp': nrm((L, GW, D), GW ** -0.5),
        'w_out': nrm((L, D, D), D ** -0.5),
        'w_ff1': nrm((L, D, D_FF), D ** -0.5),
        'w_ff2': nrm((L, D_FF, D), D_FF ** -0.5),
        'final_g': 1.0 + nrm((D,), 0.05),
    }


def reference(x, c, w_ada, b_ada, norm1_g, norm2_g, w_in, mu_shift, w0_decay, w2_decay, a0,
              w2_aaa, w2_gate, k_k, k_a, r_k, gn_w, gn_b, v0_res, w1_res, w2_res,
              ln_gmlp_w, ln_gmlp_b, w_spatial, b_spatial, w_br_rwkv, w_br_gmlp, w_out,
              w_ff1, w_ff2, final_g):
    pos = jnp.arange(GMLP_BLOCK)
    mask = (pos[:, None] // CHUNK >= pos[None, :] // CHUNK).astype(w_spatial.dtype)
    c_act = jax.nn.silu(c)
    v_first = None
    for l in range(DEPTH):
        mod = c_act @ w_ada[l] + b_ada[l]
        sh1, sc1, gt1, sh2, sc2, gt2 = jnp.split(mod, N_MOD, axis=-1)
        h = modulate(rms_norm(x, norm1_g[l]), sh1, sc1)
        p = h @ w_in[l]
        p_rw, p_gm, p_ga, p_gb = jnp.split(
            p, [RWKV_COLS, RWKV_COLS + GMLP_COLS, RWKV_COLS + GMLP_COLS + D_MODEL], axis=-1)
        vres = None if l == 0 else (v0_res[l - 1], w1_res[l - 1], w2_res[l - 1])
        y_rw, v_l = rwkv7_mix(p_rw, mu_shift[l], w0_decay[l], w2_decay[l], a0[l], w2_aaa[l],
                              w2_gate[l], k_k[l], k_a[l], r_k[l], gn_w[l], gn_b[l], v_first, vres)
        if l == 0:
            v_first = v_l
        y_gm = gmlp_mix(p_gm, ln_gmlp_w[l], ln_gmlp_b[l], w_spatial[l], b_spatial[l], mask)
        merged = (jax.nn.sigmoid(p_ga) * (y_rw @ w_br_rwkv[l])
                  + jax.nn.sigmoid(p_gb) * (y_gm @ w_br_gmlp[l]))
        x = x + gt1[:, None, :] * (merged @ w_out[l])
        h2 = modulate(rms_norm(x, norm2_g[l]), sh2, sc2)
        ff = jnp.square(jax.nn.relu(h2 @ w_ff1[l])) @ w_ff2[l]
        x = x + gt2[:, None, :] * ff
    return rms_norm(x, final_g)
```

```python
import functools

import jax
import jax.numpy as jnp
from jax import lax
from jax.experimental import pallas as pl
from jax.experimental.pallas import tpu as pltpu

F32 = jnp.float32
BF16 = jnp.bfloat16

RWKV_HEADS = 8
HEAD_DIM = 64
RWKV_WIDTH = RWKV_HEADS * HEAD_DIM
DECAY_LORA = 64
AAA_LORA = 64
GATE_LORA = 128
VRES_LORA = 32
GMLP_BLOCK = 128
GMLP_GROUPS = 4
GMLP_WIDTH = 512
STREAM_CHUNK = 64
N_MOD = 6
RMS_EPS = 1e-6
LN_EPS = 1e-5
GN_EPS = 64e-5

LANES = 128
HEADS_PER_TILE = LANES // HEAD_DIM
N_PAIRS = RWKV_HEADS // HEADS_PER_TILE
SCAN_CHUNK = 64
PAIR_ROWS = HEADS_PER_TILE * SCAN_CHUNK
VMEM_LIMIT_BYTES = 56 * 1024 * 1024

COL_LORA = 3 * RWKV_WIDTH
COL_GLORA = COL_LORA + DECAY_LORA + AAA_LORA
RWKV_COLS = COL_GLORA + GATE_LORA
COL_GATES = RWKV_COLS + 2 * GMLP_WIDTH


def _mm(a, b):
    return jnp.dot(a.astype(BF16), b.astype(BF16), preferred_element_type=F32)


def _mm_nt(a, b):
    return lax.dot_general(a.astype(BF16), b.astype(BF16), (((1,), (1,)), ((), ())),
                           preferred_element_type=F32)


def _split_bf16(x, terms):
    parts = []
    for _ in range(terms):
        p = x.astype(BF16)
        parts.append(p)
        x = x - p.astype(F32)
    return parts


def _mm_exact_rhs(a, m01, terms):
    out = None
    for p in _split_bf16(a, terms):
        t = jnp.dot(p, m01, preferred_element_type=F32)
        out = t if out is None else out + t
    return out


def _mm_exact_lhs(m01, b, terms):
    out = None
    for p in _split_bf16(b, terms):
        t = jnp.dot(m01, p, preferred_element_type=F32)
        out = t if out is None else out + t
    return out


def _gelu_exact(x):
    return 0.5 * x * (1.0 + lax.erf(x * (2.0 ** -0.5)))


def _rms_norm(x, gain):
    ms = jnp.mean(x * x, axis=-1, keepdims=True)
    return (x * lax.rsqrt(ms + RMS_EPS)) * gain


def _mod_kernel(c_ref, w_ref, b_ref, o_ref):
    c = c_ref[...]
    c_act = c * jax.nn.sigmoid(c)
    o_ref[...] = _mm(c_act, w_ref[...]) + b_ref[...]


def _modulation(c, w_ada, b_ada):
    depth, d, _ = w_ada.shape
    batch = c.shape[0]
    return pl.pallas_call(
        _mod_kernel,
        grid=(depth, N_MOD),
        in_specs=[
            pl.BlockSpec((batch, d), lambda l, j: (0, 0)),
            pl.BlockSpec((None, d, d), lambda l, j: (l, 0, j)),
            pl.BlockSpec((None, 1, d), lambda l, j: (l, 0, j)),
        ],
        out_specs=pl.BlockSpec((None, batch, d), lambda l, j: (l, 0, j)),
        out_shape=jax.ShapeDtypeStruct((depth, batch, N_MOD * d), F32),
        compiler_params=pltpu.CompilerParams(
            dimension_semantics=("arbitrary", "arbitrary"),
            vmem_limit_bytes=VMEM_LIMIT_BYTES),
    )(c, w_ada, b_ada.reshape(depth, 1, N_MOD * d))


def _stack_heads(z, head_masks):
    return jnp.concatenate([jnp.where(m, z, 0.0) for m in head_masks], axis=0)


def _scan_chunk(c, refs, consts):
    r_s, lw_s, k_s, v_s, a_s, b_s, y_s, st_ref = refs
    tri_incl, strict_pair, incl_pair2, eye_pair, head_masks = consts
    rows = pl.ds(pl.multiple_of(c * SCAN_CHUNK, SCAN_CHUNK), SCAN_CHUNK)
    lwc = lw_s[rows, :]
    cl = _mm_exact_lhs(tri_incl, lwc, 3)
    clx = cl - lwc
    cl_end = cl[SCAN_CHUNK - 1:SCAN_CHUNK, :]
    e_ncl = jnp.exp(-cl)
    e_dec = jnp.exp(cl_end - cl)
    w_end = jnp.exp(cl_end)
    a_t = a_s[rows, :] * jnp.exp(clx)
    r_t = r_s[rows, :] * jnp.exp(cl)
    bc = b_s[rows, :]
    kc = k_s[rows, :]
    b_t = bc * e_ncl
    k_t = kc * e_ncl
    b_h = bc * e_dec
    k_h = kc * e_dec
    vc = v_s[rows, :]
    for p in range(N_PAIRS):
        ls = slice(p * LANES, (p + 1) * LANES)
        st = lambda z: _stack_heads(z[:, ls], head_masks)
        x_ar = jnp.concatenate([st(a_t), st(r_t)], axis=0)
        x_bk = jnp.concatenate([st(b_t), st(k_t)], axis=0)
        x_v = st(vc)
        g = _mm_nt(x_ar, x_bk)
        l_ab = g[:PAIR_ROWS, :PAIR_ROWS] * strict_pair
        l_ak = g[:PAIR_ROWS, PAIR_ROWS:] * strict_pair
        m_r = g[PAIR_ROWS:, :] * incl_pair2
        t_inv = eye_pair + l_ab
        pw = l_ab
        n = 2
        while n < SCAN_CHUNK:
            pw = _mm(pw, pw)
            t_inv = t_inv + _mm(t_inv, pw)
            n *= 2
        s = st_ref[p]
        ar_s = _mm_nt(x_ar, s)
        u = _mm(t_inv, ar_s[:PAIR_ROWS] + _mm(l_ak, x_v))
        uv = jnp.concatenate([u, x_v], axis=0)
        y_st = ar_s[PAIR_ROWS:] + _mm(m_r, uv)
        y_s[rows, ls] = y_st[:SCAN_CHUNK] + y_st[SCAN_CHUNK:]
        x_bkh = jnp.concatenate([st(b_h), st(k_h)], axis=0)
        st_ref[p] = s * w_end[:, ls] + _mm(uv.T, x_bkh)


def _mix_kernel(has_vres, ts, *refs):
    it = iter(refs)
    x_ref, mod_ref = next(it), next(it)
    vfirst_ref = next(it) if has_vres else None
    (g1_ref, w_in_ref, mu_ref, w0_ref, w2wa_ref, a0_ref, w2g_ref, kk_ref, ka_ref, rk_ref,
     gnw_ref, gnb_ref) = (next(it) for _ in range(12))
    if has_vres:
        v0_ref, w1r_ref, w2r_ref = next(it), next(it), next(it)
    (lnw_ref, lnb_ref, ws_ref, bst_ref, wbr_rw_ref, wbr_gm_ref, w_out_ref, seg_ref) = (
        next(it) for _ in range(8))
    xo_ref = next(it)
    vo_ref = None if has_vres else next(it)
    (st_ref, carry_ref, r_s, lw_s, k_s, v_s, a_s, b_s, y_s) = (next(it) for _ in range(9))

    d = x_ref.shape[-1]
    j = pl.program_id(1)

    @pl.when(j == 0)
    def _():
        st_ref[...] = jnp.zeros_like(st_ref)
        carry_ref[...] = jnp.zeros_like(carry_ref)

    x = x_ref[...]
    mod = mod_ref[...]
    sh1, sc1, gt1 = mod[:, 0:d], mod[:, d:2 * d], mod[:, 2 * d:3 * d]
    h = (_rms_norm(x, g1_ref[...]) * (1.0 + sc1) + sh1).astype(BF16)

    p_rw = jnp.dot(h, w_in_ref[:, 0:RWKV_COLS], preferred_element_type=F32)
    row = lax.broadcasted_iota(jnp.int32, (ts, 1), 0)
    prev = jnp.where(row == 0, carry_ref[...], pltpu.roll(p_rw, 1, 0))
    carry_ref[...] = p_rw[ts - 1:ts, :]
    xs = p_rw + (prev - p_rw) * mu_ref[...]
    r = xs[:, 0:RWKV_WIDTH]
    k = xs[:, RWKV_WIDTH:2 * RWKV_WIDTH]
    v = xs[:, 2 * RWKV_WIDTH:3 * RWKV_WIDTH]
    lo = xs[:, COL_LORA:COL_GLORA]
    g_lo = xs[:, COL_GLORA:RWKV_COLS]
    lane = lax.broadcasted_iota(jnp.int32, (1, LANES), 1)
    lo_act = jnp.where(lane < DECAY_LORA, jnp.tanh(lo), lo)
    wa = _mm(lo_act, w2wa_ref[...])
    w_log = -jax.nn.softplus(-(w0_ref[...] + wa[:, 0:RWKV_WIDTH])) - 0.5
    a = jax.nn.sigmoid(a0_ref[...] + wa[:, RWKV_WIDTH:])
    gate = _mm(jax.nn.sigmoid(g_lo), w2g_ref[...])
    if has_vres:
        vmix = jax.nn.sigmoid(v0_ref[...] + _mm(_mm(v, w1r_ref[...]), w2r_ref[...]))
        v = v + (vfirst_ref[...] - v) * vmix
    else:
        vo_ref[...] = v
    seg = seg_ref[...]
    kk = k * kk_ref[...]
    kk_norm = jnp.sqrt(_mm_exact_rhs(kk * kk, seg, 2))
    kk = kk / jnp.maximum(kk_norm, 1e-12)
    k = k * (1.0 + (a - 1.0) * ka_ref[...])
    r_s[...] = r
    lw_s[...] = -jnp.exp(w_log)
    k_s[...] = k
    v_s[...] = v
    a_s[...] = -kk
    b_s[...] = kk * a
    bonus = _mm_exact_rhs(r * k * rk_ref[...], seg, 2) * v

    ci = lax.broadcasted_iota(jnp.int32, (SCAN_CHUNK, SCAN_CHUNK), 0)
    cj = lax.broadcasted_iota(jnp.int32, (SCAN_CHUNK, SCAN_CHUNK), 1)
    tri_incl = (ci >= cj).astype(BF16)
    pi = lax.broadcasted_iota(jnp.int32, (PAIR_ROWS, PAIR_ROWS), 0)
    pj = lax.broadcasted_iota(jnp.int32, (PAIR_ROWS, PAIR_ROWS), 1)
    same_head = (pi // SCAN_CHUNK) == (pj // SCAN_CHUNK)
    strict_pair = (same_head & (pi > pj)).astype(F32)
    incl_pair = (same_head & (pi >= pj)).astype(F32)
    incl_pair2 = jnp.concatenate([incl_pair, incl_pair], axis=1)
    eye_pair = (pi == pj).astype(F32)
    head_masks = [(lane // HEAD_DIM) == hh for hh in range(HEADS_PER_TILE)]
    consts = (tri_incl, strict_pair, incl_pair2, eye_pair, head_masks)
    scan_refs = (r_s, lw_s, k_s, v_s, a_s, b_s, y_s, st_ref)

    def body(c, carry):
        _scan_chunk(c, scan_refs, consts)
        return carry

    lax.fori_loop(0, ts // SCAN_CHUNK, body, 0)

    y = y_s[...]
    mean = _mm_exact_rhs(y, seg, 2) * (1.0 / HEAD_DIM)
    yc = y - mean
    var = _mm_exact_rhs(yc * yc, seg, 2) * (1.0 / HEAD_DIM)
    y = yc * lax.rsqrt(var + GN_EPS) * gnw_ref[...] + gnb_ref[...]
    y_rw = (y + bonus) * gate

    z = _gelu_exact(jnp.dot(h, w_in_ref[:, RWKV_COLS:COL_GATES], preferred_element_type=F32))
    u_g = z[:, 0:GMLP_WIDTH]
    v_g = z[:, GMLP_WIDTH:]
    mu_g = jnp.mean(v_g, axis=-1, keepdims=True)
    vc_g = v_g - mu_g
    var_g = jnp.mean(vc_g * vc_g, axis=-1, keepdims=True)
    v_g = vc_g * lax.rsqrt(var_g + LN_EPS) * lnw_ref[...] + lnb_ref[...]
    bi = lax.broadcasted_iota(jnp.int32, (GMLP_BLOCK, GMLP_BLOCK), 0)
    bj = lax.broadcasted_iota(jnp.int32, (GMLP_BLOCK, GMLP_BLOCK), 1)
    causal = (bi // STREAM_CHUNK) >= (bj // STREAM_CHUNK)
    gd = GMLP_WIDTH // GMLP_GROUPS
    bst = bst_ref[...]
    sv_cols = []
    for gi in range(GMLP_GROUPS):
        w_g = jnp.where(causal, ws_ref[gi], 0.0)
        bias = bst[:, gi:gi + 1]
        blocks = [_mm(w_g, v_g[nb * GMLP_BLOCK:(nb + 1) * GMLP_BLOCK, gi * gd:(gi + 1) * gd]) + bias
                  for nb in range(ts // GMLP_BLOCK)]
        sv_cols.append(jnp.concatenate(blocks, axis=0))
    y_gm = u_g * jnp.concatenate(sv_cols, axis=1)

    gates = jax.nn.sigmoid(jnp.dot(h, w_in_ref[:, COL_GATES:], preferred_element_type=F32))
    merged = (gates[:, 0:d] * _mm(y_rw, wbr_rw_ref[...])
              + gates[:, d:] * _mm(y_gm, wbr_gm_ref[...]))
    xo_ref[...] = x + gt1 * _mm(merged, w_out_ref[...])


def _const_spec(shape, layer=None):
    if layer is None:
        return pl.BlockSpec(shape, lambda b, j: (0,) * len(shape), pipeline_mode=pl.Buffered(1))
    return pl.BlockSpec((None,) + shape, lambda b, j: (layer,) + (0,) * len(shape),
                        pipeline_mode=pl.Buffered(1))


def _token_mix(layer, x, mod, v_first, params, ts):
    batch, seq, d = x.shape
    has_vres = layer > 0
    tile = lambda w: pl.BlockSpec((None, ts, w), lambda b, j: (b, j, 0))
    row512 = _const_spec((1, RWKV_WIDTH), layer)
    p = params
    operands = [x, mod]
    in_specs = [tile(d), pl.BlockSpec((None, 1, N_MOD * d), lambda b, j: (b, 0, 0))]
    if has_vres:
        operands.append(v_first)
        in_specs.append(tile(RWKV_WIDTH))
    operands += [p["norm1_g"], p["w_in"], p["mu_shift"], p["w0_decay"], p["w2_wa"], p["a0"],
                 p["w2_gate"], p["k_k"], p["k_a"], p["r_k"], p["gn_w"], p["gn_b"]]
    in_specs += [_const_spec((1, d), layer), _const_spec(p["w_in"].shape[1:], layer),
                 _const_spec((1, RWKV_COLS), layer), row512,
                 _const_spec(p["w2_wa"].shape[1:], layer), row512,
                 _const_spec(p["w2_gate"].shape[1:], layer), row512, row512, row512, row512, row512]
    if has_vres:
        operands += [p["v0_res"], p["w1_res"], p["w2_res"]]
        in_specs += [_const_spec((1, RWKV_WIDTH), layer - 1),
                     _const_spec(p["w1_res"].shape[1:], layer - 1),
                     _const_spec(p["w2_res"].shape[1:], layer - 1)]
    operands += [p["ln_gmlp_w"], p["ln_gmlp_b"], p["w_spatial"], p["b_spatial_t"],
                 p["w_br_rwkv"], p["w_br_gmlp"], p["w_out"], p["seg_ones"]]
    in_specs += [_const_spec((1, GMLP_WIDTH), layer), _const_spec((1, GMLP_WIDTH), layer),
                 _const_spec(p["w_spatial"].shape[1:], layer),
                 _const_spec(p["b_spatial_t"].shape[1:], layer),
                 _const_spec(p["w_br_rwkv"].shape[1:], layer),
                 _const_spec(p["w_br_gmlp"].shape[1:], layer),
                 _const_spec(p["w_out"].shape[1:], layer),
                 _const_spec(p["seg_ones"].shape)]
    out_shape = [jax.ShapeDtypeStruct((batch, seq, d), F32)]
    out_specs = [tile(d)]
    if not has_vres:
        out_shape.append(jax.ShapeDtypeStruct((batch, seq, RWKV_WIDTH), F32))
        out_specs.append(tile(RWKV_WIDTH))
    buf = pltpu.VMEM((ts, RWKV_WIDTH), F32)
    scratch = [pltpu.VMEM((N_PAIRS, LANES, LANES), F32), pltpu.VMEM((1, RWKV_COLS), F32),
               buf, buf, buf, buf, buf, buf, buf]
    outs = pl.pallas_call(
        functools.partial(_mix_kernel, has_vres, ts),
        grid=(batch, seq // ts),
        in_specs=in_specs,
        out_specs=out_specs,
        out_shape=out_shape,
        scratch_shapes=scratch,
        compiler_params=pltpu.CompilerParams(
            dimension_semantics=("arbitrary", "arbitrary"),
            vmem_limit_bytes=VMEM_LIMIT_BYTES),
    )(*operands)
    return outs if has_vres else (outs[0], outs[1])


def _ffn_kernel(final, ff_chunk, *refs):
    if final:
        x_ref, mod_ref, g2_ref, w1_ref, w2_ref, fg_ref, o_ref = refs
    else:
        x_ref, mod_ref, g2_ref, w1_ref, w2_ref, o_ref = refs
    d = x_ref.shape[-1]
    x = x_ref[...]
    mod = mod_ref[...]
    sh2, sc2, gt2 = mod[:, 3 * d:4 * d], mod[:, 4 * d:5 * d], mod[:, 5 * d:6 * d]
    h = (_rms_norm(x, g2_ref[...]) * (1.0 + sc2) + sh2).astype(BF16)
    acc = jnp.zeros_like(x)
    for c0 in range(0, w1_ref.shape[1], ff_chunk):
        t = jnp.maximum(jnp.dot(h, w1_ref[:, c0:c0 + ff_chunk], preferred_element_type=F32), 0.0)
        acc = acc + _mm(t * t, w2_ref[c0:c0 + ff_chunk, :])
    xn = x + gt2 * acc
    if final:
        xn = _rms_norm(xn, fg_ref[...])
    o_ref[...] = xn


def _channel_mix(layer, x, mod, params, final, tm, ff_chunk=1024):
    batch, seq, d = x.shape
    tile = pl.BlockSpec((None, tm, d), lambda b, j: (b, j, 0))
    operands = [x, mod, params["norm2_g"], params["w_ff1"], params["w_ff2"]]
    in_specs = [tile, pl.BlockSpec((None, 1, N_MOD * d), lambda b, j: (b, 0, 0)),
                _const_spec((1, d), layer), _const_spec(params["w_ff1"].shape[1:], layer),
                _const_spec(params["w_ff2"].shape[1:], layer)]
    if final:
        operands.append(params["final_g"])
        in_specs.append(_const_spec((1, d)))
    return pl.pallas_call(
        functools.partial(_ffn_kernel, final, ff_chunk),
        grid=(batch, seq // tm),
        in_specs=in_specs,
        out_specs=tile,
        out_shape=jax.ShapeDtypeStruct((batch, seq, d), F32),
        compiler_params=pltpu.CompilerParams(
            dimension_semantics=("arbitrary", "arbitrary"),
            vmem_limit_bytes=VMEM_LIMIT_BYTES),
    )(*operands)


def kernel(x, c, w_ada, b_ada, norm1_g, norm2_g, w_in, mu_shift, w0_decay, w2_decay, a0, w2_aaa, w2_gate, k_k, k_a, r_k, gn_w, gn_b, v0_res, w1_res, w2_res, ln_gmlp_w, ln_gmlp_b, w_spatial, b_spatial, w_br_rwkv, w_br_gmlp, w_out, w_ff1, w_ff2, final_g):
    depth = w_in.shape[0]
    batch, seq, d = x.shape
    ts = min(256, seq)
    tm = min(512, seq)
    row = lambda t: t.reshape(t.shape[0], 1, -1)
    zeros = lambda *s: jnp.zeros(s, F32)
    w2_wa = jnp.concatenate([
        jnp.concatenate([w2_decay, zeros(depth, DECAY_LORA, RWKV_WIDTH)], axis=2),
        jnp.concatenate([zeros(depth, AAA_LORA, RWKV_WIDTH), w2_aaa], axis=2)], axis=1)
    pad = LANES - VRES_LORA
    head_id = jnp.arange(RWKV_WIDTH) // HEAD_DIM
    params = dict(
        norm1_g=row(norm1_g), norm2_g=row(norm2_g), w_in=w_in.astype(BF16), mu_shift=row(mu_shift),
        w0_decay=row(w0_decay), w2_wa=w2_wa.astype(BF16), a0=row(a0), w2_gate=w2_gate.astype(BF16),
        k_k=row(k_k), k_a=row(k_a), r_k=row(r_k), gn_w=row(gn_w), gn_b=row(gn_b),
        v0_res=row(v0_res),
        w1_res=jnp.pad(w1_res, ((0, 0), (0, 0), (0, pad))).astype(BF16),
        w2_res=jnp.pad(w2_res, ((0, 0), (0, pad), (0, 0))).astype(BF16),
        ln_gmlp_w=row(ln_gmlp_w), ln_gmlp_b=row(ln_gmlp_b), w_spatial=w_spatial,
        b_spatial_t=jnp.swapaxes(b_spatial, 1, 2),
        w_br_rwkv=w_br_rwkv.astype(BF16), w_br_gmlp=w_br_gmlp.astype(BF16),
        w_out=w_out.astype(BF16), w_ff1=w_ff1.astype(BF16), w_ff2=w_ff2.astype(BF16),
        final_g=final_g.reshape(1, d),
        seg_ones=(head_id[:, None] == head_id[None, :]).astype(BF16),
    )
    mod = _modulation(c, w_ada, b_ada)
    v_first = None
    for layer in range(depth):
        mod_l = mod[layer].reshape(batch, 1, N_MOD * d)
        if layer == 0:
            x, v_first = _token_mix(layer, x, mod_l, None, params, ts)
        else:
            (x,) = _token_mix(layer, x, mod_l, v_first, params, ts)
        x = _channel_mix(layer, x, mod_l, params, layer == depth - 1, tm)
    return x
```

```python
import functools

import jax
import jax.numpy as jnp
from jax import lax
from jax.experimental import pallas as pl
from jax.experimental.pallas import tpu as pltpu

F32 = jnp.float32
BF16 = jnp.bfloat16

RWKV_HEADS = 8
HEAD_DIM = 64
RWKV_WIDTH = RWKV_HEADS * HEAD_DIM
DECAY_LORA = 64
AAA_LORA = 64
GATE_LORA = 128
VRES_LORA = 32
GMLP_BLOCK = 128
GMLP_GROUPS = 4
GMLP_WIDTH = 512
STREAM_CHUNK = 64
N_MOD = 6
RMS_EPS = 1e-6
LN_EPS = 1e-5
GN_EPS = 64e-5

LANES = 128
HEADS_PER_TILE = LANES // HEAD_DIM
N_PAIRS = RWKV_HEADS // HEADS_PER_TILE
SCAN_CHUNK = 64
PAIR_ROWS = HEADS_PER_TILE * SCAN_CHUNK
VMEM_LIMIT_BYTES = 56 * 1024 * 1024

COL_LORA = 3 * RWKV_WIDTH
COL_GLORA = COL_LORA + DECAY_LORA + AAA_LORA
RWKV_COLS = COL_GLORA + GATE_LORA
COL_GATES = RWKV_COLS + 2 * GMLP_WIDTH


def _mm(a, b):
    return jnp.dot(a.astype(BF16), b.astype(BF16), preferred_element_type=F32)


def _mm_nt(a, b):
    return lax.dot_general(a.astype(BF16), b.astype(BF16), (((1,), (1,)), ((), ())),
                           preferred_element_type=F32)


def _split_bf16(x, terms):
    parts = []
    for _ in range(terms):
        p = x.astype(BF16)
        parts.append(p)
        x = x - p.astype(F32)
    return parts


def _mm_exact_rhs(a, m01, terms):
    out = None
    for p in _split_bf16(a, terms):
        t = jnp.dot(p, m01, preferred_element_type=F32)
        out = t if out is None else out + t
    return out


def _mm_exact_lhs(m01, b, terms):
    out = None
    for p in _split_bf16(b, terms):
        t = jnp.dot(m01, p, preferred_element_type=F32)
        out = t if out is None else out + t
    return out


def _gelu_exact(x):
    return 0.5 * x * (1.0 + lax.erf(x * (2.0 ** -0.5)))


def _rms_norm(x, gain):
    ms = jnp.mean(x * x, axis=-1, keepdims=True)
    return (x * lax.rsqrt(ms + RMS_EPS)) * gain


def _mod_kernel(c_ref, w_ref, b_ref, o_ref):
    c = c_ref[...]
    c_act = c * jax.nn.sigmoid(c)
    o_ref[...] = _mm(c_act, w_ref[...]) + b_ref[...]


def _modulation(c, w_ada, b_ada):
    depth, d, _ = w_ada.shape
    batch = c.shape[0]
    return pl.pallas_call(
        _mod_kernel,
        grid=(depth, N_MOD),
        in_specs=[
            pl.BlockSpec((batch, d), lambda l, j: (0, 0)),
            pl.BlockSpec((None, d, d), lambda l, j: (l, 0, j)),
            pl.BlockSpec((None, 1, d), lambda l, j: (l, 0, j)),
        ],
        out_specs=pl.BlockSpec((None, batch, d), lambda l, j: (l, 0, j)),
        out_shape=jax.ShapeDtypeStruct((depth, batch, N_MOD * d), F32),
        compiler_params=pltpu.CompilerParams(
            dimension_semantics=("arbitrary", "arbitrary"),
            vmem_limit_bytes=VMEM_LIMIT_BYTES),
    )(c, w_ada, b_ada.reshape(depth, 1, N_MOD * d))


def _scan_tile(r, lw, k, v, a, b, st_ref, lane):
    ts = r.shape[0]
    n_chunks = ts // SCAN_CHUNK
    ti = lax.broadcasted_iota(jnp.int32, (ts, ts), 0)
    tj = lax.broadcasted_iota(jnp.int32, (ts, ts), 1)
    tri_blk = ((ti // SCAN_CHUNK == tj // SCAN_CHUNK) & (ti >= tj)).astype(BF16)
    pi = lax.broadcasted_iota(jnp.int32, (PAIR_ROWS, PAIR_ROWS), 0)
    pj = lax.broadcasted_iota(jnp.int32, (PAIR_ROWS, PAIR_ROWS), 1)
    same_head = (pi // SCAN_CHUNK) == (pj // SCAN_CHUNK)
    strict_pair = (same_head & (pi > pj)).astype(F32)
    incl_pair = (same_head & (pi >= pj)).astype(F32)
    incl_pair2 = jnp.concatenate([incl_pair, incl_pair], axis=1)
    eye_pair = (pi == pj).astype(F32)
    zero_pair = jnp.zeros((PAIR_ROWS, LANES), F32)
    head_masks = [(lane // HEAD_DIM) == hh for hh in range(HEADS_PER_TILE)]

    cl = _mm_exact_lhs(tri_blk, lw, 3)
    e_ncl = jnp.exp(-cl)
    a_t = a * jnp.exp(cl - lw)
    r_t = r * jnp.exp(cl)
    b_t = b * e_ncl
    k_t = k * e_ncl
    b_h, k_h, w_end = [], [], []
    for c in range(n_chunks):
        rs = slice(c * SCAN_CHUNK, (c + 1) * SCAN_CHUNK)
        cl_end = cl[(c + 1) * SCAN_CHUNK - 1:(c + 1) * SCAN_CHUNK, :]
        e_dec = jnp.exp(cl_end - cl[rs, :])
        b_h.append(b[rs, :] * e_dec)
        k_h.append(k[rs, :] * e_dec)
        w_end.append(jnp.exp(cl_end))

    insts = [(c, p) for c in range(n_chunks) for p in range(N_PAIRS)]

    def stack(z, c, p):
        zz = z[c * SCAN_CHUNK:(c + 1) * SCAN_CHUNK, p * LANES:(p + 1) * LANES]
        return jnp.concatenate([jnp.where(m, zz, 0.0) for m in head_masks], axis=0)

    x_a = {i: stack(a_t, *i) for i in insts}
    x_r = {i: stack(r_t, *i) for i in insts}
    x_v = {i: stack(v, *i) for i in insts}
    g = {i: _mm_nt(jnp.concatenate([x_a[i], x_r[i]], axis=0),
                   jnp.concatenate([stack(b_t, *i), stack(k_t, *i)], axis=0)) for i in insts}
    l_ab = {i: g[i][:PAIR_ROWS, :PAIR_ROWS] * strict_pair for i in insts}
    l_ak = {i: g[i][:PAIR_ROWS, PAIR_ROWS:] * strict_pair for i in insts}
    m_r = {i: g[i][PAIR_ROWS:, :] * incl_pair2 for i in insts}
    t_inv = {i: eye_pair + l_ab[i] for i in insts}
    pw = {i: _mm(l_ab[i], l_ab[i]) for i in insts}
    n = 4
    while n < SCAN_CHUNK:
        for i in insts:
            both = _mm(pw[i], jnp.concatenate([pw[i], t_inv[i]], axis=1))
            pw[i] = both[:, :PAIR_ROWS]
            t_inv[i] = t_inv[i] + both[:, PAIR_ROWS:]
        n *= 2
    for i in insts:
        t_inv[i] = t_inv[i] + _mm(pw[i], t_inv[i])
    lv = {i: _mm(l_ak[i], x_v[i]) for i in insts}
    pq = {i: _mm(t_inv[i], jnp.concatenate([x_a[i], lv[i]], axis=1)) for i in insts}
    z = {i: jnp.concatenate([pq[i], jnp.concatenate([zero_pair, x_v[i]], axis=1)], axis=0)
         for i in insts}
    mz = {i: _mm(m_r[i], z[i]) for i in insts}
    x_bkh = {(c, p): jnp.concatenate([stack(b_h[c], 0, p), stack(k_h[c], 0, p)], axis=0)
             for c, p in insts}
    zb = {i: _mm(z[i].T, x_bkh[i]) for i in insts}
    y_rows = []
    s = [st_ref[p] for p in range(N_PAIRS)]
    for c in range(n_chunks):
        y_cols = []
        for p in range(N_PAIRS):
            i = (c, p)
            y_st = _mm_nt(x_r[i] + mz[i][:, :PAIR_ROWS], s[p]) + mz[i][:, PAIR_ROWS:]
            y_cols.append(y_st[:SCAN_CHUNK] + y_st[SCAN_CHUNK:])
            s[p] = (s[p] * w_end[c][:, p * LANES:(p + 1) * LANES]
                    + _mm(s[p], zb[i][:PAIR_ROWS]) + zb[i][PAIR_ROWS:])
        y_rows.append(jnp.concatenate(y_cols, axis=1))
    for p in range(N_PAIRS):
        st_ref[p] = s[p]
    return jnp.concatenate(y_rows, axis=0)


def _mix_kernel(has_vres, ts, *refs):
    it = iter(refs)
    x_ref, mod_ref = next(it), next(it)
    vfirst_ref = next(it) if has_vres else None
    (g1_ref, w_in_ref, mu_ref, w0_ref, w2wa_ref, a0_ref, w2g_ref, kk_ref, ka_ref, rk_ref,
     gnw_ref, gnb_ref) = (next(it) for _ in range(12))
    if has_vres:
        v0_ref, w1r_ref, w2r_ref = next(it), next(it), next(it)
    (lnw_ref, lnb_ref, ws_ref, bst_ref, wbr_rw_ref, wbr_gm_ref, w_out_ref, seg_ref) = (
        next(it) for _ in range(8))
    xo_ref = next(it)
    vo_ref = None if has_vres else next(it)
    st_ref, carry_ref = next(it), next(it)

    d = x_ref.shape[-1]
    j = pl.program_id(1)

    @pl.when(j == 0)
    def _():
        st_ref[...] = jnp.zeros_like(st_ref)
        carry_ref[...] = jnp.zeros_like(carry_ref)

    x = x_ref[...]
    mod = mod_ref[...]
    sh1, sc1, gt1 = mod[:, 0:d], mod[:, d:2 * d], mod[:, 2 * d:3 * d]
    h = (_rms_norm(x, g1_ref[...]) * (1.0 + sc1) + sh1).astype(BF16)

    p_rw = jnp.dot(h, w_in_ref[:, 0:RWKV_COLS], preferred_element_type=F32)
    row = lax.broadcasted_iota(jnp.int32, (ts, 1), 0)
    prev = jnp.where(row == 0, carry_ref[...], pltpu.roll(p_rw, 1, 0))
    carry_ref[...] = p_rw[ts - 1:ts, :]
    xs = p_rw + (prev - p_rw) * mu_ref[...]
    r = xs[:, 0:RWKV_WIDTH]
    k = xs[:, RWKV_WIDTH:2 * RWKV_WIDTH]
    v = xs[:, 2 * RWKV_WIDTH:3 * RWKV_WIDTH]
    lo = xs[:, COL_LORA:COL_GLORA]
    g_lo = xs[:, COL_GLORA:RWKV_COLS]
    lane = lax.broadcasted_iota(jnp.int32, (1, LANES), 1)
    lo_act = jnp.where(lane < DECAY_LORA, jnp.tanh(lo), lo)
    wa = _mm(lo_act, w2wa_ref[...])
    w_log = -jax.nn.softplus(-(w0_ref[...] + wa[:, 0:RWKV_WIDTH])) - 0.5
    a = jax.nn.sigmoid(a0_ref[...] + wa[:, RWKV_WIDTH:])
    gate = _mm(jax.nn.sigmoid(g_lo), w2g_ref[...])
    if has_vres:
        vmix = jax.nn.sigmoid(v0_ref[...] + _mm(_mm(v, w1r_ref[...]), w2r_ref[...]))
        v = v + (vfirst_ref[...] - v) * vmix
    else:
        vo_ref[...] = v
    seg = seg_ref[...]
    kk = k * kk_ref[...]
    kk_norm = jnp.sqrt(_mm_exact_rhs(kk * kk, seg, 2))
    kk = kk / jnp.maximum(kk_norm, 1e-12)
    k = k * (1.0 + (a - 1.0) * ka_ref[...])
    bonus = _mm_exact_rhs(r * k * rk_ref[...], seg, 2) * v
    y = _scan_tile(r, -jnp.exp(w_log), k, v, -kk, kk * a, st_ref, lane)
    mean = _mm_exact_rhs(y, seg, 2) * (1.0 / HEAD_DIM)
    yc = y - mean
    var = _mm_exact_rhs(yc * yc, seg, 2) * (1.0 / HEAD_DIM)
    y = yc * lax.rsqrt(var + GN_EPS) * gnw_ref[...] + gnb_ref[...]
    y_rw = (y + bonus) * gate

    z = _gelu_exact(jnp.dot(h, w_in_ref[:, RWKV_COLS:COL_GATES], preferred_element_type=F32))
    u_g = z[:, 0:GMLP_WIDTH]
    v_g = z[:, GMLP_WIDTH:]
    mu_g = jnp.mean(v_g, axis=-1, keepdims=True)
    vc_g = v_g - mu_g
    var_g = jnp.mean(vc_g * vc_g, axis=-1, keepdims=True)
    v_g = vc_g * lax.rsqrt(var_g + LN_EPS) * lnw_ref[...] + lnb_ref[...]
    bi = lax.broadcasted_iota(jnp.int32, (GMLP_BLOCK, GMLP_BLOCK), 0)
    bj = lax.broadcasted_iota(jnp.int32, (GMLP_BLOCK, GMLP_BLOCK), 1)
    causal = (bi // STREAM_CHUNK) >= (bj // STREAM_CHUNK)
    gd = GMLP_WIDTH // GMLP_GROUPS
    bst = bst_ref[...]
    sv_cols = []
    for gi in range(GMLP_GROUPS):
        w_g = jnp.where(causal, ws_ref[gi], 0.0)
        bias = bst[:, gi:gi + 1]
        blocks = [_mm(w_g, v_g[nb * GMLP_BLOCK:(nb + 1) * GMLP_BLOCK, gi * gd:(gi + 1) * gd]) + bias
                  for nb in range(ts // GMLP_BLOCK)]
        sv_cols.append(jnp.concatenate(blocks, axis=0))
    y_gm = u_g * jnp.concatenate(sv_cols, axis=1)

    gates = jax.nn.sigmoid(jnp.dot(h, w_in_ref[:, COL_GATES:], preferred_element_type=F32))
    merged = (gates[:, 0:d] * _mm(y_rw, wbr_rw_ref[...])
              + gates[:, d:] * _mm(y_gm, wbr_gm_ref[...]))
    xo_ref[...] = x + gt1 * _mm(merged, w_out_ref[...])


def _const_spec(shape, layer=None):
    if layer is None:
        return pl.BlockSpec(shape, lambda b, j: (0,) * len(shape), pipeline_mode=pl.Buffered(1))
    return pl.BlockSpec((None,) + shape, lambda b, j: (layer,) + (0,) * len(shape),
                        pipeline_mode=pl.Buffered(1))


def _token_mix(layer, x, mod, v_first, params, ts):
    batch, seq, d = x.shape
    has_vres = layer > 0
    tile = lambda w: pl.BlockSpec((None, ts, w), lambda b, j: (b, j, 0))
    row512 = _const_spec((1, RWKV_WIDTH), layer)
    p = params
    operands = [x, mod]
    in_specs = [tile(d), pl.BlockSpec((None, 1, N_MOD * d), lambda b, j: (b, 0, 0))]
    if has_vres:
        operands.append(v_first)
        in_specs.append(tile(RWKV_WIDTH))
    operands += [p["norm1_g"], p["w_in"], p["mu_shift"], p["w0_decay"], p["w2_wa"], p["a0"],
                 p["w2_gate"], p["k_k"], p["k_a"], p["r_k"], p["gn_w"], p["gn_b"]]
    in_specs += [_const_spec((1, d), layer), _const_spec(p["w_in"].shape[1:], layer),
                 _const_spec((1, RWKV_COLS), layer), row512,
                 _const_spec(p["w2_wa"].shape[1:], layer), row512,
                 _const_spec(p["w2_gate"].shape[1:], layer), row512, row512, row512, row512, row512]
    if has_vres:
        operands += [p["v0_res"], p["w1_res"], p["w2_res"]]
        in_specs += [_const_spec((1, RWKV_WIDTH), layer - 1),
                     _const_spec(p["w1_res"].shape[1:], layer - 1),
                     _const_spec(p["w2_res"].shape[1:], layer - 1)]
    operands += [p["ln_gmlp_w"], p["ln_gmlp_b"], p["w_spatial"], p["b_spatial_t"],
                 p["w_br_rwkv"], p["w_br_gmlp"], p["w_out"], p["seg_ones"]]
    in_specs += [_const_spec((1, GMLP_WIDTH), layer), _const_spec((1, GMLP_WIDTH), layer),
                 _const_spec(p["w_spatial"].shape[1:], layer),
                 _const_spec(p["b_spatial_t"].shape[1:], layer),
                 _const_spec(p["w_br_rwkv"].shape[1:], layer),
                 _const_spec(p["w_br_gmlp"].shape[1:], layer),
                 _const_spec(p["w_out"].shape[1:], layer),
                 _const_spec(p["seg_ones"].shape)]
    out_shape = [jax.ShapeDtypeStruct((batch, seq, d), F32)]
    out_specs = [tile(d)]
    if not has_vres:
        out_shape.append(jax.ShapeDtypeStruct((batch, seq, RWKV_WIDTH), F32))
        out_specs.append(tile(RWKV_WIDTH))
    scratch = [pltpu.VMEM((N_PAIRS, LANES, LANES), F32), pltpu.VMEM((1, RWKV_COLS), F32)]
    outs = pl.pallas_call(
        functools.partial(_mix_kernel, has_vres, ts),
        grid=(batch, seq // ts),
        in_specs=in_specs,
        out_specs=out_specs,
        out_shape=out_shape,
        scratch_shapes=scratch,
        compiler_params=pltpu.CompilerParams(
            dimension_semantics=("arbitrary", "arbitrary"),
            vmem_limit_bytes=VMEM_LIMIT_BYTES),
    )(*operands)
    return outs if has_vres else (outs[0], outs[1])


def _ffn_kernel(final, ff_chunk, *refs):
    if final:
        x_ref, mod_ref, g2_ref, w1_ref, w2_ref, fg_ref, o_ref = refs
    else:
        x_ref, mod_ref, g2_ref, w1_ref, w2_ref, o_ref = refs
    d = x_ref.shape[-1]
    x = x_ref[...]
    mod = mod_ref[...]
    sh2, sc2, gt2 = mod[:, 3 * d:4 * d], mod[:, 4 * d:5 * d], mod[:, 5 * d:6 * d]
    h = (_rms_norm(x, g2_ref[...]) * (1.0 + sc2) + sh2).astype(BF16)
    acc = jnp.zeros_like(x)
    for c0 in range(0, w1_ref.shape[1], ff_chunk):
        t = jnp.maximum(jnp.dot(h, w1_ref[:, c0:c0 + ff_chunk], preferred_element_type=F32), 0.0)
        acc = acc + _mm(t * t, w2_ref[c0:c0 + ff_chunk, :])
    xn = x + gt2 * acc
    if final:
        xn = _rms_norm(xn, fg_ref[...])
    o_ref[...] = xn


def _channel_mix(layer, x, mod, params, final, tm, ff_chunk=1024):
    batch, seq, d = x.shape
    tile = pl.BlockSpec((None, tm, d), lambda b, j: (b, j, 0))
    operands = [x, mod, params["norm2_g"], params["w_ff1"], params["w_ff2"]]
    in_specs = [tile, pl.BlockSpec((None, 1, N_MOD * d), lambda b, j: (b, 0, 0)),
                _const_spec((1, d), layer), _const_spec(params["w_ff1"].shape[1:], layer),
                _const_spec(params["w_ff2"].shape[1:], layer)]
    if final:
        operands.append(params["final_g"])
        in_specs.append(_const_spec((1, d)))
    return pl.pallas_call(
        functools.partial(_ffn_kernel, final, ff_chunk),
        grid=(batch, seq // tm),
        in_specs=in_specs,
        out_specs=tile,
        out_shape=jax.ShapeDtypeStruct((batch, seq, d), F32),
        compiler_params=pltpu.CompilerParams(
            dimension_semantics=("arbitrary", "arbitrary"),
            vmem_limit_bytes=VMEM_LIMIT_BYTES),
    )(*operands)


def kernel(x, c, w_ada, b_ada, norm1_g, norm2_g, w_in, mu_shift, w0_decay, w2_decay, a0, w2_aaa, w2_gate, k_k, k_a, r_k, gn_w, gn_b, v0_res, w1_res, w2_res, ln_gmlp_w, ln_gmlp_b, w_spatial, b_spatial, w_br_rwkv, w_br_gmlp, w_out, w_ff1, w_ff2, final_g):
    depth = w_in.shape[0]
    batch, seq, d = x.shape
    ts = min(256, seq)
    tm = min(512, seq)
    row = lambda t: t.reshape(t.shape[0], 1, -1)
    zeros = lambda *s: jnp.zeros(s, F32)
    w2_wa = jnp.concatenate([
        jnp.concatenate([w2_decay, zeros(depth, DECAY_LORA, RWKV_WIDTH)], axis=2),
        jnp.concatenate([zeros(depth, AAA_LORA, RWKV_WIDTH), w2_aaa], axis=2)], axis=1)
    pad = LANES - VRES_LORA
    head_id = jnp.arange(RWKV_WIDTH) // HEAD_DIM
    params = dict(
        norm1_g=row(norm1_g), norm2_g=row(norm2_g), w_in=w_in.astype(BF16), mu_shift=row(mu_shift),
        w0_decay=row(w0_decay), w2_wa=w2_wa.astype(BF16), a0=row(a0), w2_gate=w2_gate.astype(BF16),
        k_k=row(k_k), k_a=row(k_a), r_k=row(r_k), gn_w=row(gn_w), gn_b=row(gn_b),
        v0_res=row(v0_res),
        w1_res=jnp.pad(w1_res, ((0, 0), (0, 0), (0, pad))).astype(BF16),
        w2_res=jnp.pad(w2_res, ((0, 0), (0, pad), (0, 0))).astype(BF16),
        ln_gmlp_w=row(ln_gmlp_w), ln_gmlp_b=row(ln_gmlp_b), w_spatial=w_spatial,
        b_spatial_t=jnp.swapaxes(b_spatial, 1, 2),
        w_br_rwkv=w_br_rwkv.astype(BF16), w_br_gmlp=w_br_gmlp.astype(BF16),
        w_out=w_out.astype(BF16), w_ff1=w_ff1.astype(BF16), w_ff2=w_ff2.astype(BF16),
        final_g=final_g.reshape(1, d),
        seg_ones=(head_id[:, None] == head_id[None, :]).astype(BF16),
    )
    mod = _modulation(c, w_ada, b_ada)
    v_first = None
    for layer in range(depth):
        mod_l = mod[layer].reshape(batch, 1, N_MOD * d)
        if layer == 0:
            x, v_first = _token_mix(layer, x, mod_l, None, params, ts)
        else:
            (x,) = _token_mix(layer, x, mod_l, v_first, params, ts)
        x = _channel_mix(layer, x, mod_l, params, layer == depth - 1, tm)
    return x
```

```python
import functools

import jax
import jax.numpy as jnp
from jax import lax
from jax.experimental import pallas as pl
from jax.experimental.pallas import tpu as pltpu

F32 = jnp.float32
BF16 = jnp.bfloat16

RWKV_HEADS = 8
HEAD_DIM = 64
RWKV_WIDTH = RWKV_HEADS * HEAD_DIM
DECAY_LORA = 64
AAA_LORA = 64
GATE_LORA = 128
VRES_LORA = 32
GMLP_BLOCK = 128
GMLP_GROUPS = 4
GMLP_WIDTH = 512
STREAM_CHUNK = 64
N_MOD = 6
RMS_EPS = 1e-6
LN_EPS = 1e-5
GN_EPS = 64e-5

LANES = 128
MXU_WIDTH = 256
HEADS_PER_GROUP = MXU_WIDTH // HEAD_DIM
N_GROUPS = RWKV_HEADS // HEADS_PER_GROUP
SCAN_CHUNK = 64
VMEM_LIMIT_BYTES = 56 * 1024 * 1024

COL_LORA = 3 * RWKV_WIDTH
COL_GLORA = COL_LORA + DECAY_LORA + AAA_LORA
RWKV_COLS = COL_GLORA + GATE_LORA
COL_GATES = RWKV_COLS + 2 * GMLP_WIDTH


def _mm(a, b):
    return jnp.dot(a.astype(BF16), b.astype(BF16), preferred_element_type=F32)


def _mm_nt(a, b):
    return lax.dot_general(a.astype(BF16), b.astype(BF16), (((1,), (1,)), ((), ())),
                           preferred_element_type=F32)


def _split_bf16(x, terms):
    parts = []
    for _ in range(terms):
        p = x.astype(BF16)
        parts.append(p)
        x = x - p.astype(F32)
    return parts


def _head_sums(x, seg):
    w = seg.shape[0]
    xb = x.astype(BF16)
    return jnp.concatenate(
        [jnp.dot(xb[:, c0:c0 + w], seg, preferred_element_type=F32)
         for c0 in range(0, x.shape[1], w)], axis=1)


def _mm_exact_lhs(m01, b, terms):
    out = None
    for p in _split_bf16(b, terms):
        t = jnp.dot(m01, p, preferred_element_type=F32)
        out = t if out is None else out + t
    return out


def _gelu_exact(x):
    return 0.5 * x * (1.0 + lax.erf(x * (2.0 ** -0.5)))


def _rms_norm(x, gain):
    ms = jnp.mean(x * x, axis=-1, keepdims=True)
    return (x * lax.rsqrt(ms + RMS_EPS)) * gain


def _mod_kernel(c_ref, w_ref, b_ref, o_ref):
    c = c_ref[...]
    c_act = c * jax.nn.sigmoid(c)
    o_ref[...] = _mm(c_act, w_ref[...]) + b_ref[...]


def _modulation(c, w_ada, b_ada):
    depth, d, _ = w_ada.shape
    batch = c.shape[0]
    return pl.pallas_call(
        _mod_kernel,
        grid=(depth, N_MOD),
        in_specs=[
            pl.BlockSpec((batch, d), lambda l, j: (0, 0)),
            pl.BlockSpec((None, d, d), lambda l, j: (l, 0, j)),
            pl.BlockSpec((None, 1, d), lambda l, j: (l, 0, j)),
        ],
        out_specs=pl.BlockSpec((None, batch, d), lambda l, j: (l, 0, j)),
        out_shape=jax.ShapeDtypeStruct((depth, batch, N_MOD * d), F32),
        compiler_params=pltpu.CompilerParams(
            dimension_semantics=("arbitrary", "arbitrary"),
            vmem_limit_bytes=VMEM_LIMIT_BYTES),
    )(c, w_ada, b_ada.reshape(depth, 1, N_MOD * d))


def _scan_tile(r, lw, k, v, a, b, st_ref):
    ts = r.shape[0]
    n_chunks = ts // SCAN_CHUNK
    ti = lax.broadcasted_iota(jnp.int32, (ts, ts), 0)
    tj = lax.broadcasted_iota(jnp.int32, (ts, ts), 1)
    tri_blk = ((ti // SCAN_CHUNK == tj // SCAN_CHUNK) & (ti >= tj)).astype(BF16)
    t_row = lax.broadcasted_iota(jnp.int32, (SCAN_CHUNK, MXU_WIDTH), 0)
    g_lane = lax.broadcasted_iota(jnp.int32, (SCAN_CHUNK, MXU_WIDTH), 1)
    t_col = g_lane % HEAD_DIM
    strict = (t_row > t_col).astype(F32)
    incl = (t_row >= t_col).astype(F32)
    eye = (t_row == t_col).astype(F32)
    head_of_lane = g_lane[0:1, :] // HEAD_DIM
    head_masks = [head_of_lane == hh for hh in range(HEADS_PER_GROUP)]

    def bd(x):
        xb = x.astype(BF16)
        return jnp.concatenate([jnp.where(m, xb, jnp.zeros_like(xb)) for m in head_masks], axis=0)

    def dot(x, w):
        return jnp.dot(x.astype(BF16), w, preferred_element_type=F32)

    def dot_nt(x, w):
        return lax.dot_general(x.astype(BF16), w, (((1,), (1,)), ((), ())),
                               preferred_element_type=F32)

    cl = _mm_exact_lhs(tri_blk, lw, 3)
    e_ncl = jnp.exp(-cl)
    a_t = a * jnp.exp(cl - lw)
    r_t = r * jnp.exp(cl)
    b_t = b * e_ncl
    k_t = k * e_ncl
    b_h, k_h, w_end = [], [], []
    for c in range(n_chunks):
        rs = slice(c * SCAN_CHUNK, (c + 1) * SCAN_CHUNK)
        cl_end = cl[(c + 1) * SCAN_CHUNK - 1:(c + 1) * SCAN_CHUNK, :]
        e_dec = jnp.exp(cl_end - cl[rs, :])
        b_h.append(b[rs, :] * e_dec)
        k_h.append(k[rs, :] * e_dec)
        w_end.append(jnp.exp(cl_end))

    insts = [(c, q) for c in range(n_chunks) for q in range(N_GROUPS)]
    w = MXU_WIDTH

    def part(z, c, q):
        return z[c * SCAN_CHUNK:(c + 1) * SCAN_CHUNK, q * w:(q + 1) * w]

    a_i = {i: part(a_t, *i) for i in insts}
    r_i = {i: part(r_t, *i) for i in insts}
    v_i = {i: part(v, *i) for i in insts}
    gg = {i: dot_nt(jnp.concatenate([a_i[i], r_i[i]], axis=0),
                    jnp.concatenate([bd(part(b_t, *i)), bd(part(k_t, *i))], axis=0))
          for i in insts}
    yield
    l_ab = {i: gg[i][:SCAN_CHUNK, :w] * strict for i in insts}
    l_ak = {i: gg[i][:SCAN_CHUNK, w:] * strict for i in insts}
    m_rb = {i: gg[i][SCAN_CHUNK:, :w] * incl for i in insts}
    m_rk = {i: gg[i][SCAN_CHUNK:, w:] * incl for i in insts}
    t_inv = {i: eye + l_ab[i] for i in insts}
    pw = {i: dot(l_ab[i], bd(l_ab[i])) for i in insts}
    n = 4
    while n < SCAN_CHUNK:
        yield
        for i in insts:
            both = dot(jnp.concatenate([pw[i], t_inv[i]], axis=0), bd(pw[i]))
            pw[i] = both[:SCAN_CHUNK]
            t_inv[i] = t_inv[i] + both[SCAN_CHUNK:]
        n *= 2
    yield
    for i in insts:
        t_inv[i] = t_inv[i] + dot(t_inv[i], bd(pw[i]))
    lv = {i: dot(l_ak[i], bd(v_i[i])) for i in insts}
    yield
    pq = {i: dot(t_inv[i], jnp.concatenate([bd(a_i[i]), bd(lv[i])], axis=1)) for i in insts}
    yield
    bd_pqv = {i: [bd(pq[i][:, :w]), bd(pq[i][:, w:]), bd(v_i[i])] for i in insts}
    mz = {i: dot(m_rb[i], jnp.concatenate(bd_pqv[i][:2], axis=1)) for i in insts}
    r_p = {i: r_i[i] + mz[i][:, :w] for i in insts}
    y_loc = {i: mz[i][:, w:] + dot(m_rk[i], bd_pqv[i][2]) for i in insts}
    yield
    pqv_t = {i: dot_nt(eye, jnp.concatenate(bd_pqv[i], axis=0)) for i in insts}
    yield
    bd_bh ={(c, q): bd(part(b_h[c], 0, q)) for c, q in insts}
    bd_kh = {(c, q): bd(part(k_h[c], 0, q)) for c, q in insts}
    phi_psi = {i: dot(jnp.concatenate([pqv_t[i][:, :w], pqv_t[i][:, w:2 * w]], axis=0), bd_bh[i])
               for i in insts}
    yield
    bd_phi = {i: bd(phi_psi[i][:SCAN_CHUNK]) for i in insts}
    psi = {i: phi_psi[i][SCAN_CHUNK:] + dot(pqv_t[i][:, 2 * w:], bd_kh[i]) for i in insts}
    y_rows = []
    s = [st_ref[q] for q in range(N_GROUPS)]
    for c in range(n_chunks):
        y_cols = []
        for q in range(N_GROUPS):
            i = (c, q)
            y_cols.append(dot_nt(r_p[i], bd(s[q])) + y_loc[i])
            s[q] = s[q] * w_end[c][:, q * w:(q + 1) * w] + dot(s[q], bd_phi[i]) + psi[i]
        y_rows.append(jnp.concatenate(y_cols, axis=1))
        yield
    for q in range(N_GROUPS):
        st_ref[q] = s[q]
    return jnp.concatenate(y_rows, axis=0)


def _gmlp_and_gates(h, w_in_ref, lnw_ref, lnb_ref, ws_ref, bst_ref, ts):
    cw = MXU_WIDTH
    z_cols = []
    for c0 in range(RWKV_COLS, COL_GATES, cw):
        z_cols.append(_gelu_exact(jnp.dot(h, w_in_ref[:, c0:c0 + cw], preferred_element_type=F32)))
        yield
    n_u = GMLP_WIDTH // cw
    u_g = jnp.concatenate(z_cols[:n_u], axis=1)
    v_g = jnp.concatenate(z_cols[n_u:], axis=1)
    mu_g = jnp.mean(v_g, axis=-1, keepdims=True)
    vc_g = v_g - mu_g
    var_g = jnp.mean(vc_g * vc_g, axis=-1, keepdims=True)
    v_g = vc_g * lax.rsqrt(var_g + LN_EPS) * lnw_ref[...] + lnb_ref[...]
    yield
    bi = lax.broadcasted_iota(jnp.int32, (GMLP_BLOCK, GMLP_BLOCK), 0)
    bj = lax.broadcasted_iota(jnp.int32, (GMLP_BLOCK, GMLP_BLOCK), 1)
    causal = (bi // STREAM_CHUNK) >= (bj // STREAM_CHUNK)
    gd = GMLP_WIDTH // GMLP_GROUPS
    bst = bst_ref[...]
    sv_cols = []
    for gi in range(GMLP_GROUPS):
        w_g = jnp.where(causal, ws_ref[gi], 0.0)
        bias = bst[:, gi:gi + 1]
        v_blocks = jnp.concatenate(
            [v_g[nb * GMLP_BLOCK:(nb + 1) * GMLP_BLOCK, gi * gd:(gi + 1) * gd]
             for nb in range(ts // GMLP_BLOCK)], axis=1)
        sv = _mm(w_g, v_blocks) + bias
        sv_cols.append(jnp.concatenate(
            [sv[:, nb * gd:(nb + 1) * gd] for nb in range(ts // GMLP_BLOCK)], axis=0))
    y_gm = u_g * jnp.concatenate(sv_cols, axis=1)
    yield
    gate_cols = []
    for c0 in range(COL_GATES, w_in_ref.shape[1], cw):
        gate_cols.append(jax.nn.sigmoid(
            jnp.dot(h, w_in_ref[:, c0:c0 + cw], preferred_element_type=F32)))
        yield
    return y_gm, jnp.concatenate(gate_cols, axis=1)


def _interleave(main, side):
    out = [None, None]
    live = [main, side]
    while any(g is not None for g in live):
        for n, g in enumerate(live):
            if g is not None:
                try:
                    next(g)
                except StopIteration as stop:
                    out[n], live[n] = stop.value, None
    return out


def _mix_kernel(has_vres, ts, *refs):
    it = iter(refs)
    x_ref, mod_ref = next(it), next(it)
    vfirst_ref = next(it) if has_vres else None
    (g1_ref, w_in_ref, mu_ref, w0_ref, w2wa_ref, a0_ref, w2g_ref, kk_ref, ka_ref, rk_ref,
     gnw_ref, gnb_ref) = (next(it) for _ in range(12))
    if has_vres:
        v0_ref, w1r_ref, w2r_ref = next(it), next(it), next(it)
    (lnw_ref, lnb_ref, ws_ref, bst_ref, wbr_rw_ref, wbr_gm_ref, w_out_ref, seg_ref) = (
        next(it) for _ in range(8))
    xo_ref = next(it)
    vo_ref = None if has_vres else next(it)
    st_ref, carry_ref = next(it), next(it)

    d = x_ref.shape[-1]
    j = pl.program_id(1)

    @pl.when(j == 0)
    def _():
        st_ref[...] = jnp.zeros_like(st_ref)
        carry_ref[...] = jnp.zeros_like(carry_ref)

    x = x_ref[...]
    mod = mod_ref[...]
    sh1, sc1, gt1 = mod[:, 0:d], mod[:, d:2 * d], mod[:, 2 * d:3 * d]
    h = (_rms_norm(x, g1_ref[...]) * (1.0 + sc1) + sh1).astype(BF16)

    p_rw = jnp.dot(h, w_in_ref[:, 0:RWKV_COLS], preferred_element_type=F32)
    row = lax.broadcasted_iota(jnp.int32, (ts, 1), 0)
    prev = jnp.where(row == 0, carry_ref[...], pltpu.roll(p_rw, 1, 0))
    carry_ref[...] = p_rw[ts - 1:ts, :]
    xs = p_rw + (prev - p_rw) * mu_ref[...]
    r = xs[:, 0:RWKV_WIDTH]
    k = xs[:, RWKV_WIDTH:2 * RWKV_WIDTH]
    v = xs[:, 2 * RWKV_WIDTH:3 * RWKV_WIDTH]
    lo = xs[:, COL_LORA:COL_GLORA]
    g_lo = xs[:, COL_GLORA:RWKV_COLS]
    lane = lax.broadcasted_iota(jnp.int32, (1, LANES), 1)
    lo_act = jnp.where(lane < DECAY_LORA, jnp.tanh(lo), lo)
    wa = _mm(lo_act, w2wa_ref[...])
    w_log = -jax.nn.softplus(-(w0_ref[...] + wa[:, 0:RWKV_WIDTH])) - 0.5
    a = jax.nn.sigmoid(a0_ref[...] + wa[:, RWKV_WIDTH:])
    gate = _mm(jax.nn.sigmoid(g_lo), w2g_ref[...])
    if has_vres:
        vmix = jax.nn.sigmoid(v0_ref[...] + _mm(_mm(v, w1r_ref[...]), w2r_ref[...]))
        v = v + (vfirst_ref[...] - v) * vmix
    else:
        vo_ref[...] = v
    seg = seg_ref[...]
    kk = k * kk_ref[...]
    kk_norm = jnp.sqrt(_head_sums(kk * kk, seg))
    kk = kk / jnp.maximum(kk_norm, 1e-12)
    k = k * (1.0 + (a - 1.0) * ka_ref[...])
    bonus = _head_sums(r * k * rk_ref[...], seg) * v
    y, (y_gm, gates) = _interleave(
        _scan_tile(r, -jnp.exp(w_log), k, v, -kk, kk * a, st_ref),
        _gmlp_and_gates(h, w_in_ref, lnw_ref, lnb_ref, ws_ref, bst_ref, ts))
    mean = _head_sums(y, seg) * (1.0 / HEAD_DIM)
    yc = y - mean
    var = _head_sums(yc * yc, seg) * (1.0 / HEAD_DIM)
    y = yc * lax.rsqrt(var + GN_EPS) * gnw_ref[...] + gnb_ref[...]
    y_rw = (y + bonus) * gate

    merged = (gates[:, 0:d] * _mm(y_rw, wbr_rw_ref[...])
              + gates[:, d:] * _mm(y_gm, wbr_gm_ref[...]))
    xo_ref[...] = x + gt1 * _mm(merged, w_out_ref[...])


def _const_spec(shape, layer=None):
    if layer is None:
        return pl.BlockSpec(shape, lambda b, j: (0,) * len(shape), pipeline_mode=pl.Buffered(1))
    return pl.BlockSpec((None,) + shape, lambda b, j: (layer,) + (0,) * len(shape),
                        pipeline_mode=pl.Buffered(1))


def _token_mix(layer, x, mod, v_first, params, ts):
    batch, seq, d = x.shape
    has_vres = layer > 0
    tile = lambda w: pl.BlockSpec((None, ts, w), lambda b, j: (b, j, 0))
    row512 = _const_spec((1, RWKV_WIDTH), layer)
    p = params
    operands = [x, mod]
    in_specs = [tile(d), pl.BlockSpec((None, 1, N_MOD * d), lambda b, j: (b, 0, 0))]
    if has_vres:
        operands.append(v_first)
        in_specs.append(tile(RWKV_WIDTH))
    operands += [p["norm1_g"], p["w_in"], p["mu_shift"], p["w0_decay"], p["w2_wa"], p["a0"],
                 p["w2_gate"], p["k_k"], p["k_a"], p["r_k"], p["gn_w"], p["gn_b"]]
    in_specs += [_const_spec((1, d), layer), _const_spec(p["w_in"].shape[1:], layer),
                 _const_spec((1, RWKV_COLS), layer), row512,
                 _const_spec(p["w2_wa"].shape[1:], layer), row512,
                 _const_spec(p["w2_gate"].shape[1:], layer), row512, row512, row512, row512, row512]
    if has_vres:
        operands += [p["v0_res"], p["w1_res"], p["w2_res"]]
        in_specs += [_const_spec((1, RWKV_WIDTH), layer - 1),
                     _const_spec(p["w1_res"].shape[1:], layer - 1),
                     _const_spec(p["w2_res"].shape[1:], layer - 1)]
    operands += [p["ln_gmlp_w"], p["ln_gmlp_b"], p["w_spatial"], p["b_spatial_t"],
                 p["w_br_rwkv"], p["w_br_gmlp"], p["w_out"], p["seg_ones"]]
    in_specs += [_const_spec((1, GMLP_WIDTH), layer), _const_spec((1, GMLP_WIDTH), layer),
                 _const_spec(p["w_spatial"].shape[1:], layer),
                 _const_spec(p["b_spatial_t"].shape[1:], layer),
                 _const_spec(p["w_br_rwkv"].shape[1:], layer),
                 _const_spec(p["w_br_gmlp"].shape[1:], layer),
                 _const_spec(p["w_out"].shape[1:], layer),
                 _const_spec(p["seg_ones"].shape)]
    out_shape = [jax.ShapeDtypeStruct((batch, seq, d), F32)]
    out_specs = [tile(d)]
    if not has_vres:
        out_shape.append(jax.ShapeDtypeStruct((batch, seq, RWKV_WIDTH), F32))
        out_specs.append(tile(RWKV_WIDTH))
    scratch = [pltpu.VMEM((N_GROUPS, HEAD_DIM, MXU_WIDTH), F32), pltpu.VMEM((1, RWKV_COLS), F32)]
    outs = pl.pallas_call(
        functools.partial(_mix_kernel, has_vres, ts),
        grid=(batch, seq // ts),
        in_specs=in_specs,
        out_specs=out_specs,
        out_shape=out_shape,
        scratch_shapes=scratch,
        compiler_params=pltpu.CompilerParams(
            dimension_semantics=("arbitrary", "arbitrary"),
            vmem_limit_bytes=VMEM_LIMIT_BYTES),
    )(*operands)
    return outs if has_vres else (outs[0], outs[1])


def _ffn_kernel(final, ff_chunk, *refs):
    if final:
        x_ref, mod_ref, g2_ref, w1_ref, w2_ref, fg_ref, o_ref = refs
    else:
        x_ref, mod_ref, g2_ref, w1_ref, w2_ref, o_ref = refs
    d = x_ref.shape[-1]
    x = x_ref[...]
    mod = mod_ref[...]
    sh2, sc2, gt2 = mod[:, 3 * d:4 * d], mod[:, 4 * d:5 * d], mod[:, 5 * d:6 * d]
    h = (_rms_norm(x, g2_ref[...]) * (1.0 + sc2) + sh2).astype(BF16)
    acc = jnp.zeros_like(x)
    for c0 in range(0, w1_ref.shape[1], ff_chunk):
        t = jnp.maximum(jnp.dot(h, w1_ref[:, c0:c0 + ff_chunk], preferred_element_type=F32), 0.0)
        acc = acc + _mm(t * t, w2_ref[c0:c0 + ff_chunk, :])
    xn = x + gt2 * acc
    if final:
        xn = _rms_norm(xn, fg_ref[...])
    o_ref[...] = xn


def _channel_mix(layer, x, mod, params, final, tm, ff_chunk=1024):
    batch, seq, d = x.shape
    tile = pl.BlockSpec((None, tm, d), lambda b, j: (b, j, 0))
    operands = [x, mod, params["norm2_g"], params["w_ff1"], params["w_ff2"]]
    in_specs = [tile, pl.BlockSpec((None, 1, N_MOD * d), lambda b, j: (b, 0, 0)),
                _const_spec((1, d), layer), _const_spec(params["w_ff1"].shape[1:], layer),
                _const_spec(params["w_ff2"].shape[1:], layer)]
    if final:
        operands.append(params["final_g"])
        in_specs.append(_const_spec((1, d)))
    return pl.pallas_call(
        functools.partial(_ffn_kernel, final, ff_chunk),
        grid=(batch, seq // tm),
        in_specs=in_specs,
        out_specs=tile,
        out_shape=jax.ShapeDtypeStruct((batch, seq, d), F32),
        compiler_params=pltpu.CompilerParams(
            dimension_semantics=("arbitrary", "arbitrary"),
            vmem_limit_bytes=VMEM_LIMIT_BYTES),
    )(*operands)


def kernel(x, c, w_ada, b_ada, norm1_g, norm2_g, w_in, mu_shift, w0_decay, w2_decay, a0, w2_aaa, w2_gate, k_k, k_a, r_k, gn_w, gn_b, v0_res, w1_res, w2_res, ln_gmlp_w, ln_gmlp_b, w_spatial, b_spatial, w_br_rwkv, w_br_gmlp, w_out, w_ff1, w_ff2, final_g):
    depth = w_in.shape[0]
    batch, seq, d = x.shape
    ts = min(256, seq)
    tm = min(512, seq)
    row = lambda t: t.reshape(t.shape[0], 1, -1)
    zeros = lambda *s: jnp.zeros(s, F32)
    w2_wa = jnp.concatenate([
        jnp.concatenate([w2_decay, zeros(depth, DECAY_LORA, RWKV_WIDTH)], axis=2),
        jnp.concatenate([zeros(depth, AAA_LORA, RWKV_WIDTH), w2_aaa], axis=2)], axis=1)
    pad = LANES - VRES_LORA
    head_id = jnp.arange(MXU_WIDTH) // HEAD_DIM
    params = dict(
        norm1_g=row(norm1_g), norm2_g=row(norm2_g), w_in=w_in.astype(BF16), mu_shift=row(mu_shift),
        w0_decay=row(w0_decay), w2_wa=w2_wa.astype(BF16), a0=row(a0), w2_gate=w2_gate.astype(BF16),
        k_k=row(k_k), k_a=row(k_a), r_k=row(r_k), gn_w=row(gn_w), gn_b=row(gn_b),
        v0_res=row(v0_res),
        w1_res=jnp.pad(w1_res, ((0, 0), (0, 0), (0, pad))).astype(BF16),
        w2_res=jnp.pad(w2_res, ((0, 0), (0, pad), (0, 0))).astype(BF16),
        ln_gmlp_w=row(ln_gmlp_w), ln_gmlp_b=row(ln_gmlp_b), w_spatial=w_spatial,
        b_spatial_t=jnp.swapaxes(b_spatial, 1, 2),
        w_br_rwkv=w_br_rwkv.astype(BF16), w_br_gmlp=w_br_gmlp.astype(BF16),
        w_out=w_out.astype(BF16), w_ff1=w_ff1.astype(BF16), w_ff2=w_ff2.astype(BF16),
        final_g=final_g.reshape(1, d),
        seg_ones=(head_id[:, None] == head_id[None, :]).astype(BF16),
    )
    mod = _modulation(c, w_ada, b_ada)
    v_first = None
    for layer in range(depth):
        mod_l = mod[layer].reshape(batch, 1, N_MOD * d)
        if layer == 0:
            x, v_first = _token_mix(layer, x, mod_l, None, params, ts)
        else:
            (x,) = _token_mix(layer, x, mod_l, v_first, params, ts)
        x = _channel_mix(layer, x, mod_l, params, layer == depth - 1, tm)
    return x
```

```python
import functools

import jax
import jax.numpy as jnp
from jax import lax
from jax.experimental import pallas as pl
from jax.experimental.pallas import tpu as pltpu

F32 = jnp.float32
BF16 = jnp.bfloat16

RWKV_HEADS = 8
HEAD_DIM = 64
RWKV_WIDTH = RWKV_HEADS * HEAD_DIM
DECAY_LORA = 64
AAA_LORA = 64
GATE_LORA = 128
VRES_LORA = 32
GMLP_BLOCK = 128
GMLP_GROUPS = 4
GMLP_WIDTH = 512
STREAM_CHUNK = 64
N_MOD = 6
RMS_EPS = 1e-6
LN_EPS = 1e-5
GN_EPS = 64e-5

LANES = 128
MXU_WIDTH = 256
HEADS_PER_GROUP = MXU_WIDTH // HEAD_DIM
N_GROUPS = RWKV_HEADS // HEADS_PER_GROUP
SCAN_CHUNK = 64
VMEM_LIMIT_BYTES = 56 * 1024 * 1024

COL_LORA = 3 * RWKV_WIDTH
COL_GLORA = COL_LORA + DECAY_LORA + AAA_LORA
RWKV_COLS = COL_GLORA + GATE_LORA
COL_GATES = RWKV_COLS + 2 * GMLP_WIDTH


def _mm(a, b):
    return jnp.dot(a.astype(BF16), b.astype(BF16), preferred_element_type=F32)


def _mm_nt(a, b):
    return lax.dot_general(a.astype(BF16), b.astype(BF16), (((1,), (1,)), ((), ())),
                           preferred_element_type=F32)


def _split_bf16(x, terms):
    parts = []
    for _ in range(terms):
        p = x.astype(BF16)
        parts.append(p)
        x = x - p.astype(F32)
    return parts


def _head_sums(x, seg):
    w = seg.shape[0]
    xb = x.astype(BF16)
    return jnp.concatenate(
        [jnp.dot(xb[:, c0:c0 + w], seg, preferred_element_type=F32)
         for c0 in range(0, x.shape[1], w)], axis=1)


def _mm_exact_rhs(a, m01, terms):
    out = None
    for p in _split_bf16(a, terms):
        t = jnp.dot(p, m01, preferred_element_type=F32)
        out = t if out is None else out + t
    return out


def _mm_exact_lhs(m01, b, terms):
    out = None
    for p in _split_bf16(b, terms):
        t = jnp.dot(m01, p, preferred_element_type=F32)
        out = t if out is None else out + t
    return out


def _gelu_exact(x):
    return 0.5 * x * (1.0 + lax.erf(x * (2.0 ** -0.5)))


def _rms_norm(x, gain):
    ms = jnp.mean(x * x, axis=-1, keepdims=True)
    return (x * lax.rsqrt(ms + RMS_EPS)) * gain


def _mod_kernel(c_ref, w_ref, b_ref, o_ref):
    c = c_ref[...]
    c_act = c * jax.nn.sigmoid(c)
    o_ref[...] = _mm(c_act, w_ref[...]) + b_ref[...]


def _modulation(c, w_ada, b_ada):
    depth, d, _ = w_ada.shape
    batch = c.shape[0]
    return pl.pallas_call(
        _mod_kernel,
        grid=(depth, N_MOD),
        in_specs=[
            pl.BlockSpec((batch, d), lambda l, j: (0, 0)),
            pl.BlockSpec((None, d, d), lambda l, j: (l, 0, j)),
            pl.BlockSpec((None, 1, d), lambda l, j: (l, 0, j)),
        ],
        out_specs=pl.BlockSpec((None, batch, d), lambda l, j: (l, 0, j)),
        out_shape=jax.ShapeDtypeStruct((depth, batch, N_MOD * d), F32),
        compiler_params=pltpu.CompilerParams(
            dimension_semantics=("arbitrary", "arbitrary"),
            vmem_limit_bytes=VMEM_LIMIT_BYTES),
    )(c, w_ada, b_ada.reshape(depth, 1, N_MOD * d))


def _scan_tile(r, lw, k, v, a, b, seg, st_ref):
    ts = r.shape[0]
    n_chunks = ts // SCAN_CHUNK
    w = MXU_WIDTH
    ti = lax.broadcasted_iota(jnp.int32, (ts, ts), 0)
    tj = lax.broadcasted_iota(jnp.int32, (ts, ts), 1)
    tri_blk = ((ti // SCAN_CHUNK == tj // SCAN_CHUNK) & (ti >= tj)).astype(BF16)
    t_row = lax.broadcasted_iota(jnp.int32, (SCAN_CHUNK, MXU_WIDTH), 0)
    g_lane = lax.broadcasted_iota(jnp.int32, (SCAN_CHUNK, MXU_WIDTH), 1)
    t_col = g_lane % HEAD_DIM
    strict = (t_row > t_col).astype(F32)
    incl = (t_row >= t_col).astype(F32)
    eye = (t_row == t_col).astype(F32)
    head_of_lane = g_lane[0:1, :] // HEAD_DIM
    head_masks = [head_of_lane == hh for hh in range(HEADS_PER_GROUP)]

    def bd(x):
        xb = x.astype(BF16)
        return jnp.concatenate([jnp.where(m, xb, jnp.zeros_like(xb)) for m in head_masks], axis=0)

    def dot(x, wgt):
        return jnp.dot(x.astype(BF16), wgt, preferred_element_type=F32)

    def dot_nt(x, wgt):
        return lax.dot_general(x.astype(BF16), wgt, (((1,), (1,)), ((), ())),
                               preferred_element_type=F32)

    def part(z, c, q):
        return z[c * SCAN_CHUNK:(c + 1) * SCAN_CHUNK, q * w:(q + 1) * w]

    cl = _mm_exact_lhs(tri_blk, lw, 3)
    e_ncl = jnp.exp(-cl)
    a_t = a * jnp.exp(cl - lw)
    r_t = r * jnp.exp(cl)
    b_t = b * e_ncl
    k_t = k * e_ncl
    w_diag = [jnp.concatenate([eye, eye], axis=1)
              * jnp.exp(cl[(c + 1) * SCAN_CHUNK - 1:(c + 1) * SCAN_CHUNK, :]) for c in range(n_chunks)]
    w_diag = jnp.concatenate(w_diag, axis=0)
    w_col = [_mm_exact_rhs(w_diag[:, q * w:(q + 1) * w], seg, 3) for q in range(N_GROUPS)]

    insts = [(c, q) for c in range(n_chunks) for q in range(N_GROUPS)]
    a_i = {i: part(a_t, *i) for i in insts}
    r_i = {i: part(r_t, *i) for i in insts}
    v_i = {i: part(v, *i) for i in insts}
    w_i = {(c, q): w_col[q][c * SCAN_CHUNK:(c + 1) * SCAN_CHUNK, :] for c, q in insts}
    gg = {i: dot_nt(jnp.concatenate([a_i[i], r_i[i], eye], axis=0),
                    jnp.concatenate([bd(part(b_t, *i)), bd(part(k_t, *i))], axis=0))
          for i in insts}
    yield
    l_ab = {i: gg[i][:SCAN_CHUNK, :w] * strict for i in insts}
    l_ak = {i: gg[i][:SCAN_CHUNK, w:] * strict for i in insts}
    m_rb = {i: gg[i][SCAN_CHUNK:2 * SCAN_CHUNK, :w] * incl for i in insts}
    m_rk = {i: gg[i][SCAN_CHUNK:2 * SCAN_CHUNK, w:] * incl for i in insts}
    bh_t = {i: gg[i][2 * SCAN_CHUNK:, :w] * w_i[i] for i in insts}
    kh_t = {i: gg[i][2 * SCAN_CHUNK:, w:] * w_i[i] for i in insts}
    t_inv = {i: eye + l_ab[i] for i in insts}
    pw = {i: dot(l_ab[i], bd(l_ab[i])) for i in insts}
    n = 4
    while n < SCAN_CHUNK:
        yield
        for i in insts:
            both = dot(jnp.concatenate([pw[i], t_inv[i]], axis=0), bd(pw[i]))
            pw[i] = both[:SCAN_CHUNK]
            t_inv[i] = t_inv[i] + both[SCAN_CHUNK:]
        n *= 2
    yield
    for i in insts:
        t_inv[i] = t_inv[i] + dot(t_inv[i], bd(pw[i]))
    xv = {i: dot(jnp.concatenate([l_ak[i], m_rk[i], kh_t[i]], axis=0), bd(v_i[i])) for i in insts}
    yield
    pq = {i: dot(t_inv[i], jnp.concatenate([bd(a_i[i]), bd(xv[i][:SCAN_CHUNK])], axis=1))
          for i in insts}
    yield
    xpq = {i: dot(jnp.concatenate([m_rb[i], bh_t[i]], axis=0),
                  jnp.concatenate([bd(pq[i][:, :w]), bd(pq[i][:, w:])], axis=1)) for i in insts}
    yield
    r_phi = {i: jnp.concatenate([r_i[i] + xpq[i][:SCAN_CHUNK, :w], xpq[i][SCAN_CHUNK:, :w]], axis=0)
             for i in insts}
    y_loc = {i: xpq[i][:SCAN_CHUNK, w:] + xv[i][SCAN_CHUNK:2 * SCAN_CHUNK] for i in insts}
    psi_t = {i: xpq[i][SCAN_CHUNK:, w:] + xv[i][2 * SCAN_CHUNK:] for i in insts}
    y_rows = []
    s_t = [st_ref[q] for q in range(N_GROUPS)]
    for c in range(n_chunks):
        y_cols = []
        for q in range(N_GROUPS):
            i = (c, q)
            both = dot(r_phi[i], bd(s_t[q]))
            y_cols.append(both[:SCAN_CHUNK] + y_loc[i])
            s_t[q] = w_i[i] * s_t[q] + both[SCAN_CHUNK:] + psi_t[i]
        y_rows.append(jnp.concatenate(y_cols, axis=1))
        yield
    for q in range(N_GROUPS):
        st_ref[q] = s_t[q]
    return jnp.concatenate(y_rows, axis=0)


def _gmlp_and_gates(h, w_in_ref, lnw_ref, lnb_ref, ws_ref, bst_ref, ts):
    cw = MXU_WIDTH
    z_cols = []
    for c0 in range(RWKV_COLS, COL_GATES, cw):
        z_cols.append(_gelu_exact(jnp.dot(h, w_in_ref[:, c0:c0 + cw], preferred_element_type=F32)))
        yield
    n_u = GMLP_WIDTH // cw
    u_g = jnp.concatenate(z_cols[:n_u], axis=1)
    v_g = jnp.concatenate(z_cols[n_u:], axis=1)
    mu_g = jnp.mean(v_g, axis=-1, keepdims=True)
    vc_g = v_g - mu_g
    var_g = jnp.mean(vc_g * vc_g, axis=-1, keepdims=True)
    v_g = vc_g * lax.rsqrt(var_g + LN_EPS) * lnw_ref[...] + lnb_ref[...]
    yield
    bi = lax.broadcasted_iota(jnp.int32, (GMLP_BLOCK, GMLP_BLOCK), 0)
    bj = lax.broadcasted_iota(jnp.int32, (GMLP_BLOCK, GMLP_BLOCK), 1)
    causal = (bi // STREAM_CHUNK) >= (bj // STREAM_CHUNK)
    gd = GMLP_WIDTH // GMLP_GROUPS
    bst = bst_ref[...]
    sv_cols = []
    for gi in range(GMLP_GROUPS):
        w_g = jnp.where(causal, ws_ref[gi], 0.0)
        bias = bst[:, gi:gi + 1]
        v_blocks = jnp.concatenate(
            [v_g[nb * GMLP_BLOCK:(nb + 1) * GMLP_BLOCK, gi * gd:(gi + 1) * gd]
             for nb in range(ts // GMLP_BLOCK)], axis=1)
        sv = _mm(w_g, v_blocks) + bias
        sv_cols.append(jnp.concatenate(
            [sv[:, nb * gd:(nb + 1) * gd] for nb in range(ts // GMLP_BLOCK)], axis=0))
    y_gm = u_g * jnp.concatenate(sv_cols, axis=1)
    yield
    gate_cols = []
    for c0 in range(COL_GATES, w_in_ref.shape[1], cw):
        gate_cols.append(jax.nn.sigmoid(
            jnp.dot(h, w_in_ref[:, c0:c0 + cw], preferred_element_type=F32)))
        yield
    return y_gm, jnp.concatenate(gate_cols, axis=1)


def _interleave(main, side):
    out = [None, None]
    live = [main, side]
    while any(g is not None for g in live):
        for n, g in enumerate(live):
            if g is not None:
                try:
                    next(g)
                except StopIteration as stop:
                    out[n], live[n] = stop.value, None
    return out


def _mix_kernel(has_vres, ts, *refs):
    it = iter(refs)
    x_ref, mod_ref = next(it), next(it)
    vfirst_ref = next(it) if has_vres else None
    (g1_ref, w_in_ref, mu_ref, w0_ref, w2wa_ref, a0_ref, w2g_ref, kk_ref, ka_ref, rk_ref,
     gnw_ref, gnb_ref) = (next(it) for _ in range(12))
    if has_vres:
        v0_ref, w1r_ref, w2r_ref = next(it), next(it), next(it)
    (lnw_ref, lnb_ref, ws_ref, bst_ref, wbr_rw_ref, wbr_gm_ref, w_out_ref, seg_ref) = (
        next(it) for _ in range(8))
    xo_ref = next(it)
    vo_ref = None if has_vres else next(it)
    st_ref, carry_ref = next(it), next(it)

    d = x_ref.shape[-1]
    j = pl.program_id(1)

    @pl.when(j == 0)
    def _():
        st_ref[...] = jnp.zeros_like(st_ref)
        carry_ref[...] = jnp.zeros_like(carry_ref)

    x = x_ref[...]
    mod = mod_ref[...]
    sh1, sc1, gt1 = mod[:, 0:d], mod[:, d:2 * d], mod[:, 2 * d:3 * d]
    h = (_rms_norm(x, g1_ref[...]) * (1.0 + sc1) + sh1).astype(BF16)

    p_rw = jnp.dot(h, w_in_ref[:, 0:RWKV_COLS], preferred_element_type=F32)
    row = lax.broadcasted_iota(jnp.int32, (ts, 1), 0)
    prev = jnp.where(row == 0, carry_ref[...], pltpu.roll(p_rw, 1, 0))
    carry_ref[...] = p_rw[ts - 1:ts, :]
    xs = p_rw + (prev - p_rw) * mu_ref[...]
    r = xs[:, 0:RWKV_WIDTH]
    k = xs[:, RWKV_WIDTH:2 * RWKV_WIDTH]
    v = xs[:, 2 * RWKV_WIDTH:3 * RWKV_WIDTH]
    lo = xs[:, COL_LORA:COL_GLORA]
    g_lo = xs[:, COL_GLORA:RWKV_COLS]
    lane = lax.broadcasted_iota(jnp.int32, (1, LANES), 1)
    lo_act = jnp.where(lane < DECAY_LORA, jnp.tanh(lo), lo)
    wa = _mm(lo_act, w2wa_ref[...])
    w_log = -jax.nn.softplus(-(w0_ref[...] + wa[:, 0:RWKV_WIDTH])) - 0.5
    a = jax.nn.sigmoid(a0_ref[...] + wa[:, RWKV_WIDTH:])
    gate = _mm(jax.nn.sigmoid(g_lo), w2g_ref[...])
    if has_vres:
        vmix = jax.nn.sigmoid(v0_ref[...] + _mm(_mm(v, w1r_ref[...]), w2r_ref[...]))
        v = v + (vfirst_ref[...] - v) * vmix
    else:
        vo_ref[...] = v
    seg = seg_ref[...]
    kk = k * kk_ref[...]
    kk_norm = jnp.sqrt(_head_sums(kk * kk, seg))
    kk = kk / jnp.maximum(kk_norm, 1e-12)
    k = k * (1.0 + (a - 1.0) * ka_ref[...])
    bonus = _head_sums(r * k * rk_ref[...], seg) * v
    y, (y_gm, gates) = _interleave(
        _scan_tile(r, -jnp.exp(w_log), k, v, -kk, kk * a, seg, st_ref),
        _gmlp_and_gates(h, w_in_ref, lnw_ref, lnb_ref, ws_ref, bst_ref, ts))
    mean = _head_sums(y, seg) * (1.0 / HEAD_DIM)
    yc = y - mean
    var = _head_sums(yc * yc, seg) * (1.0 / HEAD_DIM)
    y = yc * lax.rsqrt(var + GN_EPS) * gnw_ref[...] + gnb_ref[...]
    y_rw = (y + bonus) * gate

    merged = (gates[:, 0:d] * _mm(y_rw, wbr_rw_ref[...])
              + gates[:, d:] * _mm(y_gm, wbr_gm_ref[...]))
    xo_ref[...] = x + gt1 * _mm(merged, w_out_ref[...])


def _const_spec(shape, layer=None):
    if layer is None:
        return pl.BlockSpec(shape, lambda b, j: (0,) * len(shape), pipeline_mode=pl.Buffered(1))
    return pl.BlockSpec((None,) + shape, lambda b, j: (layer,) + (0,) * len(shape),
                        pipeline_mode=pl.Buffered(1))


def _token_mix(layer, x, mod, v_first, params, ts):
    batch, seq, d = x.shape
    has_vres = layer > 0
    tile = lambda w: pl.BlockSpec((None, ts, w), lambda b, j: (b, j, 0))
    row512 = _const_spec((1, RWKV_WIDTH), layer)
    p = params
    operands = [x, mod]
    in_specs = [tile(d), pl.BlockSpec((None, 1, N_MOD * d), lambda b, j: (b, 0, 0))]
    if has_vres:
        operands.append(v_first)
        in_specs.append(tile(RWKV_WIDTH))
    operands += [p["norm1_g"], p["w_in"], p["mu_shift"], p["w0_decay"], p["w2_wa"], p["a0"],
                 p["w2_gate"], p["k_k"], p["k_a"], p["r_k"], p["gn_w"], p["gn_b"]]
    in_specs += [_const_spec((1, d), layer), _const_spec(p["w_in"].shape[1:], layer),
                 _const_spec((1, RWKV_COLS), layer), row512,
                 _const_spec(p["w2_wa"].shape[1:], layer), row512,
                 _const_spec(p["w2_gate"].shape[1:], layer), row512, row512, row512, row512, row512]
    if has_vres:
        operands += [p["v0_res"], p["w1_res"], p["w2_res"]]
        in_specs += [_const_spec((1, RWKV_WIDTH), layer - 1),
                     _const_spec(p["w1_res"].shape[1:], layer - 1),
                     _const_spec(p["w2_res"].shape[1:], layer - 1)]
    operands += [p["ln_gmlp_w"], p["ln_gmlp_b"], p["w_spatial"], p["b_spatial_t"],
                 p["w_br_rwkv"], p["w_br_gmlp"], p["w_out"], p["seg_ones"]]
    in_specs += [_const_spec((1, GMLP_WIDTH), layer), _const_spec((1, GMLP_WIDTH), layer),
                 _const_spec(p["w_spatial"].shape[1:], layer),
                 _const_spec(p["b_spatial_t"].shape[1:], layer),
                 _const_spec(p["w_br_rwkv"].shape[1:], layer),
                 _const_spec(p["w_br_gmlp"].shape[1:], layer),
                 _const_spec(p["w_out"].shape[1:], layer),
                 _const_spec(p["seg_ones"].shape)]
    out_shape = [jax.ShapeDtypeStruct((batch, seq, d), F32)]
    out_specs = [tile(d)]
    if not has_vres:
        out_shape.append(jax.ShapeDtypeStruct((batch, seq, RWKV_WIDTH), F32))
        out_specs.append(tile(RWKV_WIDTH))
    scratch = [pltpu.VMEM((N_GROUPS, HEAD_DIM, MXU_WIDTH), F32), pltpu.VMEM((1, RWKV_COLS), F32)]
    outs = pl.pallas_call(
        functools.partial(_mix_kernel, has_vres, ts),
        grid=(batch, seq // ts),
        in_specs=in_specs,
        out_specs=out_specs,
        out_shape=out_shape,
        scratch_shapes=scratch,
        compiler_params=pltpu.CompilerParams(
            dimension_semantics=("arbitrary", "arbitrary"),
            vmem_limit_bytes=VMEM_LIMIT_BYTES),
    )(*operands)
    return outs if has_vres else (outs[0], outs[1])


def _ffn_kernel(final, ff_chunk, *refs):
    if final:
        x_ref, mod_ref, g2_ref, w1_ref, w2_ref, fg_ref, o_ref = refs
    else:
        x_ref, mod_ref, g2_ref, w1_ref, w2_ref, o_ref = refs
    d = x_ref.shape[-1]
    x = x_ref[...]
    mod = mod_ref[...]
    sh2, sc2, gt2 = mod[:, 3 * d:4 * d], mod[:, 4 * d:5 * d], mod[:, 5 * d:6 * d]
    h = (_rms_norm(x, g2_ref[...]) * (1.0 + sc2) + sh2).astype(BF16)
    acc = jnp.zeros_like(x)
    for c0 in range(0, w1_ref.shape[1], ff_chunk):
        t = jnp.maximum(jnp.dot(h, w1_ref[:, c0:c0 + ff_chunk], preferred_element_type=F32), 0.0)
        acc = acc + _mm(t * t, w2_ref[c0:c0 + ff_chunk, :])
    xn = x + gt2 * acc
    if final:
        xn = _rms_norm(xn, fg_ref[...])
    o_ref[...] = xn


def _channel_mix(layer, x, mod, params, final, tm, ff_chunk=1024):
    batch, seq, d = x.shape
    tile = pl.BlockSpec((None, tm, d), lambda b, j: (b, j, 0))
    operands = [x, mod, params["norm2_g"], params["w_ff1"], params["w_ff2"]]
    in_specs = [tile, pl.BlockSpec((None, 1, N_MOD * d), lambda b, j: (b, 0, 0)),
                _const_spec((1, d), layer), _const_spec(params["w_ff1"].shape[1:], layer),
                _const_spec(params["w_ff2"].shape[1:], layer)]
    if final:
        operands.append(params["final_g"])
        in_specs.append(_const_spec((1, d)))
    return pl.pallas_call(
        functools.partial(_ffn_kernel, final, ff_chunk),
        grid=(batch, seq // tm),
        in_specs=in_specs,
        out_specs=tile,
        out_shape=jax.ShapeDtypeStruct((batch, seq, d), F32),
        compiler_params=pltpu.CompilerParams(
            dimension_semantics=("arbitrary", "arbitrary"),
            vmem_limit_bytes=VMEM_LIMIT_BYTES),
    )(*operands)


def kernel(x, c, w_ada, b_ada, norm1_g, norm2_g, w_in, mu_shift, w0_decay, w2_decay, a0, w2_aaa, w2_gate, k_k, k_a, r_k, gn_w, gn_b, v0_res, w1_res, w2_res, ln_gmlp_w, ln_gmlp_b, w_spatial, b_spatial, w_br_rwkv, w_br_gmlp, w_out, w_ff1, w_ff2, final_g):
    depth = w_in.shape[0]
    batch, seq, d = x.shape
    ts = min(256, seq)
    tm = min(512, seq)
    row = lambda t: t.reshape(t.shape[0], 1, -1)
    zeros = lambda *s: jnp.zeros(s, F32)
    w2_wa = jnp.concatenate([
        jnp.concatenate([w2_decay, zeros(depth, DECAY_LORA, RWKV_WIDTH)], axis=2),
        jnp.concatenate([zeros(depth, AAA_LORA, RWKV_WIDTH), w2_aaa], axis=2)], axis=1)
    pad = LANES - VRES_LORA
    head_id = jnp.arange(MXU_WIDTH) // HEAD_DIM
    params = dict(
        norm1_g=row(norm1_g), norm2_g=row(norm2_g), w_in=w_in.astype(BF16), mu_shift=row(mu_shift),
        w0_decay=row(w0_decay), w2_wa=w2_wa.astype(BF16), a0=row(a0), w2_gate=w2_gate.astype(BF16),
        k_k=row(k_k), k_a=row(k_a), r_k=row(r_k), gn_w=row(gn_w), gn_b=row(gn_b),
        v0_res=row(v0_res),
        w1_res=jnp.pad(w1_res, ((0, 0), (0, 0), (0, pad))).astype(BF16),
        w2_res=jnp.pad(w2_res, ((0, 0), (0, pad), (0, 0))).astype(BF16),
        ln_gmlp_w=row(ln_gmlp_w), ln_gmlp_b=row(ln_gmlp_b), w_spatial=w_spatial,
        b_spatial_t=jnp.swapaxes(b_spatial, 1, 2),
        w_br_rwkv=w_br_rwkv.astype(BF16), w_br_gmlp=w_br_gmlp.astype(BF16),
        w_out=w_out.astype(BF16), w_ff1=w_ff1.astype(BF16), w_ff2=w_ff2.astype(BF16),
        final_g=final_g.reshape(1, d),
        seg_ones=(head_id[:, None] == head_id[None, :]).astype(BF16),
    )
    mod = _modulation(c, w_ada, b_ada)
    v_first = None
    for layer in range(depth):
        mod_l = mod[layer].reshape(batch, 1, N_MOD * d)
        if layer == 0:
            x, v_first = _token_mix(layer, x, mod_l, None, params, ts)
        else:
            (x,) = _token_mix(layer, x, mod_l, v_first, params, ts)
        x = _channel_mix(layer, x, mod_l, params, layer == depth - 1, tm)
    return x
```

```python
import functools

import jax
import jax.numpy as jnp
from jax import lax
from jax.experimental import pallas as pl
from jax.experimental.pallas import tpu as pltpu

F32 = jnp.float32
BF16 = jnp.bfloat16

RWKV_HEADS = 8
HEAD_DIM = 64
RWKV_WIDTH = RWKV_HEADS * HEAD_DIM
DECAY_LORA = 64
AAA_LORA = 64
GATE_LORA = 128
VRES_LORA = 32
GMLP_BLOCK = 128
GMLP_GROUPS = 4
GMLP_WIDTH = 512
STREAM_CHUNK = 64
N_MOD = 6
RMS_EPS = 1e-6
LN_EPS = 1e-5
GN_EPS = 64e-5

LANES = 128
MXU_WIDTH = 256
HEADS_PER_GROUP = MXU_WIDTH // HEAD_DIM
N_GROUPS = RWKV_HEADS // HEADS_PER_GROUP
SCAN_CHUNK = 64
SIDE_COLS_PER_STEP = 3 * MXU_WIDTH
VMEM_LIMIT_BYTES =56 * 1024 * 1024

COL_LORA = 3 * RWKV_WIDTH
COL_GLORA = COL_LORA + DECAY_LORA + AAA_LORA
RWKV_COLS = COL_GLORA + GATE_LORA
COL_GATES = RWKV_COLS + 2 * GMLP_WIDTH


def _mm(a, b):
    return jnp.dot(a.astype(BF16), b.astype(BF16), preferred_element_type=F32)


def _mm_nt(a, b):
    return lax.dot_general(a.astype(BF16), b.astype(BF16), (((1,), (1,)), ((), ())),
                           preferred_element_type=F32)


def _split_bf16(x, terms):
    parts = []
    for _ in range(terms):
        p = x.astype(BF16)
        parts.append(p)
        x = x - p.astype(F32)
    return parts


def _head_sums(x, seg):
    w = seg.shape[0]
    xb = x.astype(BF16)
    return jnp.concatenate(
        [jnp.dot(xb[:, c0:c0 + w], seg, preferred_element_type=F32)
         for c0 in range(0, x.shape[1], w)], axis=1)


def _mm_exact_rhs(a, m01, terms):
    out = None
    for p in _split_bf16(a, terms):
        t = jnp.dot(p, m01, preferred_element_type=F32)
        out = t if out is None else out + t
    return out


def _mm_exact_lhs(m01, b, terms):
    out = None
    for p in _split_bf16(b, terms):
        t = jnp.dot(m01, p, preferred_element_type=F32)
        out = t if out is None else out + t
    return out


def _gelu_exact(x):
    return 0.5 * x * (1.0 + lax.erf(x * (2.0 ** -0.5)))


def _rms_norm(x, gain):
    ms = jnp.mean(x * x, axis=-1, keepdims=True)
    return (x * lax.rsqrt(ms + RMS_EPS)) * gain


def _mod_kernel(c_ref, w_ref, b_ref, o_ref):
    c = c_ref[...]
    c_act = c * jax.nn.sigmoid(c)
    o_ref[...] = _mm(c_act, w_ref[...]) + b_ref[...]


def _modulation(c, w_ada, b_ada):
    depth, d, _ = w_ada.shape
    batch = c.shape[0]
    return pl.pallas_call(
        _mod_kernel,
        grid=(depth, N_MOD),
        in_specs=[
            pl.BlockSpec((batch, d), lambda l, j: (0, 0)),
            pl.BlockSpec((None, d, d), lambda l, j: (l, 0, j)),
            pl.BlockSpec((None, 1, d), lambda l, j: (l, 0, j)),
        ],
        out_specs=pl.BlockSpec((None, batch, d), lambda l, j: (l, 0, j)),
        out_shape=jax.ShapeDtypeStruct((depth, batch, N_MOD * d), F32),
        compiler_params=pltpu.CompilerParams(
            dimension_semantics=("arbitrary", "arbitrary"),
            vmem_limit_bytes=VMEM_LIMIT_BYTES),
    )(c, w_ada, b_ada.reshape(depth, 1, N_MOD * d))


def _scan_tile(r, lw, k, v, a, b, seg, st_ref):
    ts = r.shape[0]
    n_chunks = ts // SCAN_CHUNK
    w = MXU_WIDTH
    ti = lax.broadcasted_iota(jnp.int32, (ts, ts), 0)
    tj = lax.broadcasted_iota(jnp.int32, (ts, ts), 1)
    tri_blk = ((ti // SCAN_CHUNK == tj // SCAN_CHUNK) & (ti >= tj)).astype(BF16)
    t_row = lax.broadcasted_iota(jnp.int32, (SCAN_CHUNK, MXU_WIDTH), 0)
    g_lane = lax.broadcasted_iota(jnp.int32, (SCAN_CHUNK, MXU_WIDTH), 1)
    t_col = g_lane % HEAD_DIM
    strict = (t_row > t_col).astype(F32)
    incl = (t_row >= t_col).astype(F32)
    eye = (t_row == t_col).astype(F32)
    head_of_lane = g_lane[0:1, :] // HEAD_DIM
    head_masks = [head_of_lane == hh for hh in range(HEADS_PER_GROUP)]

    def bd(x):
        xb = x.astype(BF16)
        return jnp.concatenate([jnp.where(m, xb, jnp.zeros_like(xb)) for m in head_masks], axis=0)

    def dot(x, wgt):
        return jnp.dot(x.astype(BF16), wgt, preferred_element_type=F32)

    def dot_nt(x, wgt):
        return lax.dot_general(x.astype(BF16), wgt, (((1,), (1,)), ((), ())),
                               preferred_element_type=F32)

    def part(z, c, q):
        return z[c * SCAN_CHUNK:(c + 1) * SCAN_CHUNK, q * w:(q + 1) * w]

    cl = _mm_exact_lhs(tri_blk, lw, 3)
    e_ncl = jnp.exp(-cl)
    a_t = a * jnp.exp(cl - lw)
    r_t = r * jnp.exp(cl)
    b_t = b * e_ncl
    k_t = k * e_ncl
    w_diag = [jnp.concatenate([eye, eye], axis=1)
              * jnp.exp(cl[(c + 1) * SCAN_CHUNK - 1:(c + 1) * SCAN_CHUNK, :]) for c in range(n_chunks)]
    w_diag = jnp.concatenate(w_diag, axis=0)
    w_col = [_mm_exact_rhs(w_diag[:, q * w:(q + 1) * w], seg, 3) for q in range(N_GROUPS)]

    insts = [(c, q) for c in range(n_chunks) for q in range(N_GROUPS)]
    a_i = {i: part(a_t, *i) for i in insts}
    r_i = {i: part(r_t, *i) for i in insts}
    v_i = {i: part(v, *i) for i in insts}
    w_i = {(c, q): w_col[q][c * SCAN_CHUNK:(c + 1) * SCAN_CHUNK, :] for c, q in insts}
    gg = {i: dot_nt(jnp.concatenate([a_i[i], r_i[i], eye], axis=0),
                    jnp.concatenate([bd(part(b_t, *i)), bd(part(k_t, *i))], axis=0))
          for i in insts}
    yield
    l_ab = {i: gg[i][:SCAN_CHUNK, :w] * strict for i in insts}
    l_ak = {i: gg[i][:SCAN_CHUNK, w:] * strict for i in insts}
    m_rb = {i: gg[i][SCAN_CHUNK:2 * SCAN_CHUNK, :w] * incl for i in insts}
    m_rk = {i: gg[i][SCAN_CHUNK:2 * SCAN_CHUNK, w:] * incl for i in insts}
    bh_t = {i: gg[i][2 * SCAN_CHUNK:, :w] * w_i[i] for i in insts}
    kh_t = {i: gg[i][2 * SCAN_CHUNK:, w:] * w_i[i] for i in insts}
    t_inv = {i: eye + l_ab[i] for i in insts}
    pw = {i: dot(l_ab[i], bd(l_ab[i])) for i in insts}
    n = 4
    while n < SCAN_CHUNK:
        yield
        for i in insts:
            both = dot(jnp.concatenate([pw[i], t_inv[i]], axis=0), bd(pw[i]))
            pw[i] = both[:SCAN_CHUNK]
            t_inv[i] = t_inv[i] + both[SCAN_CHUNK:]
        n *= 2
    yield
    for i in insts:
        t_inv[i] = t_inv[i] + dot(t_inv[i], bd(pw[i]))
    xv = {i: dot(jnp.concatenate([l_ak[i], m_rk[i], kh_t[i]], axis=0), bd(v_i[i])) for i in insts}
    yield
    pq = {i: dot(t_inv[i], jnp.concatenate([bd(a_i[i]), bd(xv[i][:SCAN_CHUNK])], axis=1))
          for i in insts}
    yield
    xpq = {i: dot(jnp.concatenate([m_rb[i], bh_t[i]], axis=0),
                  jnp.concatenate([bd(pq[i][:, :w]), bd(pq[i][:, w:])], axis=1)) for i in insts}
    yield
    r_phi = {i: jnp.concatenate([r_i[i] + xpq[i][:SCAN_CHUNK, :w], xpq[i][SCAN_CHUNK:, :w]], axis=0)
             for i in insts}
    y_loc = {i: xpq[i][:SCAN_CHUNK, w:] + xv[i][SCAN_CHUNK:2 * SCAN_CHUNK] for i in insts}
    psi_t = {i: xpq[i][SCAN_CHUNK:, w:] + xv[i][2 * SCAN_CHUNK:] for i in insts}
    y_rows = []
    s_t = [st_ref[q] for q in range(N_GROUPS)]
    for c in range(n_chunks):
        y_cols = []
        for q in range(N_GROUPS):
            i = (c, q)
            both = dot(r_phi[i], bd(s_t[q]))
            y_cols.append(both[:SCAN_CHUNK] + y_loc[i])
            s_t[q] = w_i[i] * s_t[q] + both[SCAN_CHUNK:] + psi_t[i]
        y_rows.append(jnp.concatenate(y_cols, axis=1))
        yield
    for q in range(N_GROUPS):
        st_ref[q] = s_t[q]
    return jnp.concatenate(y_rows, axis=0)


def _gmlp_and_gates(h, w_in_ref, lnw_ref, lnb_ref, ws_ref, bst_ref, wbr_gm_ref, ts):
    cw = SIDE_COLS_PER_STEP
    p_cols = []
    for c0 in range(RWKV_COLS, w_in_ref.shape[1], cw):
        p_cols.append(jnp.dot(h, w_in_ref[:, c0:c0 + cw], preferred_element_type=F32))
        yield
    p_side = jnp.concatenate(p_cols, axis=1)
    u_g = _gelu_exact(p_side[:, 0:GMLP_WIDTH])
    yield
    v_g = _gelu_exact(p_side[:, GMLP_WIDTH:2 * GMLP_WIDTH])
    yield
    mu_g = jnp.mean(v_g, axis=-1, keepdims=True)
    vc_g = v_g - mu_g
    var_g = jnp.mean(vc_g * vc_g, axis=-1, keepdims=True)
    v_g = vc_g * lax.rsqrt(var_g + LN_EPS) * lnw_ref[...] + lnb_ref[...]
    yield
    bi = lax.broadcasted_iota(jnp.int32, (GMLP_BLOCK, GMLP_BLOCK), 0)
    bj = lax.broadcasted_iota(jnp.int32, (GMLP_BLOCK, GMLP_BLOCK), 1)
    causal = (bi // STREAM_CHUNK) >= (bj // STREAM_CHUNK)
    gd = GMLP_WIDTH // GMLP_GROUPS
    bst = bst_ref[...]
    sv_cols = []
    for gi in range(GMLP_GROUPS):
        w_g = jnp.where(causal, ws_ref[gi], 0.0)
        bias = bst[:, gi:gi + 1]
        v_blocks = jnp.concatenate(
            [v_g[nb * GMLP_BLOCK:(nb + 1) * GMLP_BLOCK, gi * gd:(gi + 1) * gd]
             for nb in range(ts // GMLP_BLOCK)], axis=1)
        sv = _mm(w_g, v_blocks) + bias
        sv_cols.append(jnp.concatenate(
            [sv[:, nb * gd:(nb + 1) * gd] for nb in range(ts // GMLP_BLOCK)], axis=0))
    y_gm = u_g * jnp.concatenate(sv_cols, axis=1)
    yield
    gate_cols = []
    for c0 in range(2 * GMLP_WIDTH, p_side.shape[1], GMLP_WIDTH):
        gate_cols.append(jax.nn.sigmoid(p_side[:, c0:c0 + GMLP_WIDTH]))
        yield
    gates = jnp.concatenate(gate_cols, axis=1)
    d = gates.shape[1] // 2
    yield
    br_cols = []
    for c0 in range(0, d, MXU_WIDTH):
        br_cols.append(_mm(y_gm, wbr_gm_ref[:, c0:c0 + MXU_WIDTH]))
        yield
    return gates[:, 0:d], gates[:, d:] * jnp.concatenate(br_cols, axis=1)


def _interleave(main, side):
    out = [None, None]
    live = [main, side]
    while any(g is not None for g in live):
        for n, g in enumerate(live):
            if g is not None:
                try:
                    next(g)
                except StopIteration as stop:
                    out[n], live[n] = stop.value, None
    return out


def _mix_kernel(has_vres, ts, *refs):
    it = iter(refs)
    x_ref, mod_ref = next(it), next(it)
    vfirst_ref = next(it) if has_vres else None
    (g1_ref, w_in_ref, mu_ref, w0_ref, w2wa_ref, a0_ref, w2g_ref, kk_ref, ka_ref, rk_ref,
     gnw_ref, gnb_ref) = (next(it) for _ in range(12))
    if has_vres:
        v0_ref, w1r_ref, w2r_ref = next(it), next(it), next(it)
    (lnw_ref, lnb_ref, ws_ref, bst_ref, wbr_rw_ref, wbr_gm_ref, w_out_ref, seg_ref) = (
        next(it) for _ in range(8))
    xo_ref = next(it)
    vo_ref = None if has_vres else next(it)
    st_ref, carry_ref = next(it), next(it)

    d = x_ref.shape[-1]
    j = pl.program_id(1)

    @pl.when(j == 0)
    def _():
        st_ref[...] = jnp.zeros_like(st_ref)
        carry_ref[...] = jnp.zeros_like(carry_ref)

    x = x_ref[...]
    mod = mod_ref[...]
    sh1, sc1, gt1 = mod[:, 0:d], mod[:, d:2 * d], mod[:, 2 * d:3 * d]
    h = (_rms_norm(x, g1_ref[...]) * (1.0 + sc1) + sh1).astype(BF16)

    seg = seg_ref[...]

    def rwkv_branch():
        p_rw = jnp.dot(h, w_in_ref[:, 0:RWKV_COLS], preferred_element_type=F32)
        yield
        row = lax.broadcasted_iota(jnp.int32, (ts, 1), 0)
        prev = jnp.where(row == 0, carry_ref[...], pltpu.roll(p_rw, 1, 0))
        carry_ref[...] = p_rw[ts - 1:ts, :]
        xs = p_rw + (prev - p_rw) * mu_ref[...]
        r = xs[:, 0:RWKV_WIDTH]
        k = xs[:, RWKV_WIDTH:2 * RWKV_WIDTH]
        v = xs[:, 2 * RWKV_WIDTH:3 * RWKV_WIDTH]
        lo = xs[:, COL_LORA:COL_GLORA]
        g_lo = xs[:, COL_GLORA:RWKV_COLS]
        yield
        lane = lax.broadcasted_iota(jnp.int32, (1, LANES), 1)
        lo_act = jnp.where(lane < DECAY_LORA, jnp.tanh(lo), lo)
        wa = _mm(lo_act, w2wa_ref[...])
        w_log = -jax.nn.softplus(-(w0_ref[...] + wa[:, 0:RWKV_WIDTH])) - 0.5
        a = jax.nn.sigmoid(a0_ref[...] + wa[:, RWKV_WIDTH:])
        gate = _mm(jax.nn.sigmoid(g_lo), w2g_ref[...])
        yield
        if has_vres:
            vmix = jax.nn.sigmoid(v0_ref[...] + _mm(_mm(v, w1r_ref[...]), w2r_ref[...]))
            v = v + (vfirst_ref[...] - v) * vmix
        else:
            vo_ref[...] = v
        kk = k * kk_ref[...]
        kk_norm = jnp.sqrt(_head_sums(kk * kk, seg))
        kk = kk / jnp.maximum(kk_norm, 1e-12)
        k = k * (1.0 + (a - 1.0) * ka_ref[...])
        bonus = _head_sums(r * k * rk_ref[...], seg) * v
        yield
        y = yield from _scan_tile(r, -jnp.exp(w_log), k, v, -kk, kk * a, seg, st_ref)
        return y, bonus, gate

    (y, bonus, gate), (gate_rw, gm_term) = _interleave(
        rwkv_branch(),
        _gmlp_and_gates(h, w_in_ref, lnw_ref, lnb_ref, ws_ref, bst_ref, wbr_gm_ref, ts))
    mean = _head_sums(y, seg) * (1.0 / HEAD_DIM)
    yc = y - mean
    var = _head_sums(yc * yc, seg) * (1.0 / HEAD_DIM)
    y = yc * lax.rsqrt(var + GN_EPS) * gnw_ref[...] + gnb_ref[...]
    y_rw = (y + bonus) * gate

    merged = gate_rw * _mm(y_rw, wbr_rw_ref[...]) + gm_term
    xo_ref[...] = x + gt1 * _mm(merged, w_out_ref[...])


def _const_spec(shape, layer=None):
    if layer is None:
        return pl.BlockSpec(shape, lambda b, j: (0,) * len(shape), pipeline_mode=pl.Buffered(1))
    return pl.BlockSpec((None,) + shape, lambda b, j: (layer,) + (0,) * len(shape),
                        pipeline_mode=pl.Buffered(1))


def _token_mix(layer, x, mod, v_first, params, ts):
    batch, seq, d = x.shape
    has_vres = layer > 0
    tile = lambda w: pl.BlockSpec((None, ts, w), lambda b, j: (b, j, 0))
    row512 = _const_spec((1, RWKV_WIDTH), layer)
    p = params
    operands = [x, mod]
    in_specs = [tile(d), pl.BlockSpec((None, 1, N_MOD * d), lambda b, j: (b, 0, 0))]
    if has_vres:
        operands.append(v_first)
        in_specs.append(tile(RWKV_WIDTH))
    operands += [p["norm1_g"], p["w_in"], p["mu_shift"], p["w0_decay"], p["w2_wa"], p["a0"],
                 p["w2_gate"], p["k_k"], p["k_a"], p["r_k"], p["gn_w"], p["gn_b"]]
    in_specs += [_const_spec((1, d), layer), _const_spec(p["w_in"].shape[1:], layer),
                 _const_spec((1, RWKV_COLS), layer), row512,
                 _const_spec(p["w2_wa"].shape[1:], layer), row512,
                 _const_spec(p["w2_gate"].shape[1:], layer), row512, row512, row512, row512, row512]
    if has_vres:
        operands += [p["v0_res"], p["w1_res"], p["w2_res"]]
        in_specs += [_const_spec((1, RWKV_WIDTH), layer - 1),
                     _const_spec(p["w1_res"].shape[1:], layer - 1),
                     _const_spec(p["w2_res"].shape[1:], layer - 1)]
    operands += [p["ln_gmlp_w"], p["ln_gmlp_b"], p["w_spatial"], p["b_spatial_t"],
                 p["w_br_rwkv"], p["w_br_gmlp"], p["w_out"], p["seg_ones"]]
    in_specs += [_const_spec((1, GMLP_WIDTH), layer), _const_spec((1, GMLP_WIDTH), layer),
                 _const_spec(p["w_spatial"].shape[1:], layer),
                 _const_spec(p["b_spatial_t"].shape[1:], layer),
                 _const_spec(p["w_br_rwkv"].shape[1:], layer),
                 _const_spec(p["w_br_gmlp"].shape[1:], layer),
                 _const_spec(p["w_out"].shape[1:], layer),
                 _const_spec(p["seg_ones"].shape)]
    out_shape = [jax.ShapeDtypeStruct((batch, seq, d), F32)]
    out_specs = [tile(d)]
    if not has_vres:
        out_shape.append(jax.ShapeDtypeStruct((batch, seq, RWKV_WIDTH), F32))
        out_specs.append(tile(RWKV_WIDTH))
    scratch = [pltpu.VMEM((N_GROUPS, HEAD_DIM, MXU_WIDTH), F32), pltpu.VMEM((1, RWKV_COLS), F32)]
    outs = pl.pallas_call(
        functools.partial(_mix_kernel, has_vres, ts),
        grid=(batch, seq // ts),
        in_specs=in_specs,
        out_specs=out_specs,
        out_shape=out_shape,
        scratch_shapes=scratch,
        compiler_params=pltpu.CompilerParams(
            dimension_semantics=("arbitrary", "arbitrary"),
            vmem_limit_bytes=VMEM_LIMIT_BYTES),
    )(*operands)
    return outs if has_vres else (outs[0], outs[1])


def _ffn_kernel(final, ff_chunk, *refs):
    if final:
        x_ref, mod_ref, g2_ref, w1_ref, w2_ref, fg_ref, o_ref = refs
    else:
        x_ref, mod_ref, g2_ref, w1_ref, w2_ref, o_ref = refs
    d = x_ref.shape[-1]
    x = x_ref[...]
    mod = mod_ref[...]
    sh2, sc2, gt2 = mod[:, 3 * d:4 * d], mod[:, 4 * d:5 * d], mod[:, 5 * d:6 * d]
    h = (_rms_norm(x, g2_ref[...]) * (1.0 + sc2) + sh2).astype(BF16)
    acc = jnp.zeros_like(x)
    for c0 in range(0, w1_ref.shape[1], ff_chunk):
        t = jnp.maximum(jnp.dot(h, w1_ref[:, c0:c0 + ff_chunk], preferred_element_type=F32), 0.0)
        acc = acc + _mm(t * t, w2_ref[c0:c0 + ff_chunk, :])
    xn = x + gt2 * acc
    if final:
        xn = _rms_norm(xn, fg_ref[...])
    o_ref[...] = xn


def _channel_mix(layer, x, mod, params, final, tm, ff_chunk=1024):
    batch, seq, d = x.shape
    tile = pl.BlockSpec((None, tm, d), lambda b, j: (b, j, 0))
    operands = [x, mod, params["norm2_g"], params["w_ff1"], params["w_ff2"]]
    in_specs = [tile, pl.BlockSpec((None, 1, N_MOD * d), lambda b, j: (b, 0, 0)),
                _const_spec((1, d), layer), _const_spec(params["w_ff1"].shape[1:], layer),
                _const_spec(params["w_ff2"].shape[1:], layer)]
    if final:
        operands.append(params["final_g"])
        in_specs.append(_const_spec((1, d)))
    return pl.pallas_call(
        functools.partial(_ffn_kernel, final, ff_chunk),
        grid=(batch, seq // tm),
        in_specs=in_specs,
        out_specs=tile,
        out_shape=jax.ShapeDtypeStruct((batch, seq, d), F32),
        compiler_params=pltpu.CompilerParams(
            dimension_semantics=("arbitrary", "arbitrary"),
            vmem_limit_bytes=VMEM_LIMIT_BYTES),
    )(*operands)


def kernel(x, c, w_ada, b_ada, norm1_g, norm2_g, w_in, mu_shift, w0_decay, w2_decay, a0, w2_aaa, w2_gate, k_k, k_a, r_k, gn_w, gn_b, v0_res, w1_res, w2_res, ln_gmlp_w, ln_gmlp_b, w_spatial, b_spatial, w_br_rwkv, w_br_gmlp, w_out, w_ff1, w_ff2, final_g):
    depth = w_in.shape[0]
    batch, seq, d = x.shape
    ts = min(256, seq)
    tm = min(512, seq)
    row = lambda t: t.reshape(t.shape[0], 1, -1)
    zeros = lambda *s: jnp.zeros(s, F32)
    w2_wa = jnp.concatenate([
        jnp.concatenate([w2_decay, zeros(depth, DECAY_LORA, RWKV_WIDTH)], axis=2),
        jnp.concatenate([zeros(depth, AAA_LORA, RWKV_WIDTH), w2_aaa], axis=2)], axis=1)
    pad = LANES - VRES_LORA
    head_id = jnp.arange(MXU_WIDTH) // HEAD_DIM
    params = dict(
        norm1_g=row(norm1_g), norm2_g=row(norm2_g), w_in=w_in.astype(BF16), mu_shift=row(mu_shift),
        w0_decay=row(w0_decay), w2_wa=w2_wa.astype(BF16), a0=row(a0), w2_gate=w2_gate.astype(BF16),
        k_k=row(k_k), k_a=row(k_a), r_k=row(r_k), gn_w=row(gn_w), gn_b=row(gn_b),
        v0_res=row(v0_res),
        w1_res=jnp.pad(w1_res, ((0, 0), (0, 0), (0, pad))).astype(BF16),
        w2_res=jnp.pad(w2_res, ((0, 0), (0, pad), (0, 0))).astype(BF16),
        ln_gmlp_w=row(ln_gmlp_w), ln_gmlp_b=row(ln_gmlp_b), w_spatial=w_spatial,
        b_spatial_t=jnp.swapaxes(b_spatial, 1, 2),
        w_br_rwkv=w_br_rwkv.astype(BF16), w_br_gmlp=w_br_gmlp.astype(BF16),
        w_out=w_out.astype(BF16), w_ff1=w_ff1.astype(BF16), w_ff2=w_ff2.astype(BF16),
        final_g=final_g.reshape(1, d),
        seg_ones=(head_id[:, None] == head_id[None, :]).astype(BF16),
    )
    mod = _modulation(c, w_ada, b_ada)
    v_first = None
    for layer in range(depth):
        mod_l = mod[layer].reshape(batch, 1, N_MOD * d)
        if layer == 0:
            x, v_first = _token_mix(layer, x, mod_l, None, params, ts)
        else:
            (x,) = _token_mix(layer, x, mod_l, v_first, params, ts)
        x = _channel_mix(layer, x, mod_l, params, layer == depth - 1, tm)
    return x
```

```python
import functools

import jax
import jax.numpy as jnp
from jax import lax
from jax.experimental import pallas as pl
from jax.experimental.pallas import tpu as pltpu

F32 = jnp.float32
BF16 = jnp.bfloat16

RWKV_HEADS = 8
HEAD_DIM = 64
RWKV_WIDTH = RWKV_HEADS * HEAD_DIM
DECAY_LORA = 64
AAA_LORA = 64
GATE_LORA = 128
VRES_LORA = 32
GMLP_BLOCK = 128
GMLP_GROUPS = 4
GMLP_WIDTH = 512
STREAM_CHUNK = 64
N_MOD = 6
RMS_EPS = 1e-6
LN_EPS = 1e-5
GN_EPS = 64e-5
KK_NORM_FLOOR = 1e-12
EXACT_TERMS = 2
MIX_TOKENS = 256
FFN_TOKENS = 512
FFN_CHUNK = 1024

LANES = 128
MXU_WIDTH = 256
HEADS_PER_GROUP = MXU_WIDTH // HEAD_DIM
N_GROUPS = RWKV_HEADS // HEADS_PER_GROUP
SCAN_CHUNK = 64
SIDE_COLS_PER_STEP = 3 * MXU_WIDTH
VMEM_LIMIT_BYTES =56 * 1024 * 1024

COL_LORA = 3 * RWKV_WIDTH
COL_GLORA = COL_LORA + DECAY_LORA + AAA_LORA
RWKV_COLS = COL_GLORA + GATE_LORA
COL_GATES = RWKV_COLS + 2 * GMLP_WIDTH


def _mm(a, b):
    return jnp.dot(a.astype(BF16), b.astype(BF16), preferred_element_type=F32)


def _split_bf16(x, terms):
    parts = []
    for _ in range(terms):
        p = x.astype(BF16)
        parts.append(p)
        x = x - p.astype(F32)
    return parts


def _head_sums(x, seg):
    w = seg.shape[0]
    xb = x.astype(BF16)
    return jnp.concatenate(
        [jnp.dot(xb[:, c0:c0 + w], seg, preferred_element_type=F32)
         for c0 in range(0, x.shape[1], w)], axis=1)


def _mm_exact_rhs(a, m01, terms):
    out = None
    for p in _split_bf16(a, terms):
        t = jnp.dot(p, m01, preferred_element_type=F32)
        out = t if out is None else out + t
    return out


def _mm_exact_lhs(m01, b, terms):
    out = None
    for p in _split_bf16(b, terms):
        t = jnp.dot(m01, p, preferred_element_type=F32)
        out = t if out is None else out + t
    return out


def _gelu_exact(x):
    return 0.5 * x * (1.0 + lax.erf(x * (2.0 ** -0.5)))


def _rms_norm(x, gain):
    ms = jnp.mean(x * x, axis=-1, keepdims=True)
    return (x * lax.rsqrt(ms + RMS_EPS)) * gain


def _mod_kernel(c_ref, w_ref, b_ref, o_ref):
    c = c_ref[...]
    c_act = c * jax.nn.sigmoid(c)
    o_ref[...] = _mm(c_act, w_ref[...]) + b_ref[...]


def _modulation(c, w_ada, b_ada):
    depth, d, _ = w_ada.shape
    batch = c.shape[0]
    return pl.pallas_call(
        _mod_kernel,
        grid=(depth, N_MOD),
        in_specs=[
            pl.BlockSpec((batch, d), lambda l, j: (0, 0)),
            pl.BlockSpec((None, d, d), lambda l, j: (l, 0, j)),
            pl.BlockSpec((None, 1, d), lambda l, j: (l, 0, j)),
        ],
        out_specs=pl.BlockSpec((None, batch, d), lambda l, j: (l, 0, j)),
        out_shape=jax.ShapeDtypeStruct((depth, batch, N_MOD * d), F32),
        compiler_params=pltpu.CompilerParams(
            dimension_semantics=("arbitrary", "arbitrary"),
            vmem_limit_bytes=VMEM_LIMIT_BYTES),
    )(c, w_ada, b_ada.reshape(depth, 1, N_MOD * d))


def _scan_tile(r, lw, k, v, a, b, seg, st_ref):
    ts = r.shape[0]
    n_chunks = ts // SCAN_CHUNK
    w = MXU_WIDTH
    ti = lax.broadcasted_iota(jnp.int32, (ts, ts), 0)
    tj = lax.broadcasted_iota(jnp.int32, (ts, ts), 1)
    tri_blk = ((ti // SCAN_CHUNK == tj // SCAN_CHUNK) & (ti >= tj)).astype(BF16)
    t_row = lax.broadcasted_iota(jnp.int32, (SCAN_CHUNK, MXU_WIDTH), 0)
    g_lane = lax.broadcasted_iota(jnp.int32, (SCAN_CHUNK, MXU_WIDTH), 1)
    t_col = g_lane % HEAD_DIM
    strict = (t_row > t_col).astype(F32)
    incl = (t_row >= t_col).astype(F32)
    eye = (t_row == t_col).astype(F32)
    head_of_lane = g_lane[0:1, :] // HEAD_DIM
    head_masks = [head_of_lane == hh for hh in range(HEADS_PER_GROUP)]

    def bd(x):
        xb = x.astype(BF16)
        return jnp.concatenate([jnp.where(m, xb, jnp.zeros_like(xb)) for m in head_masks], axis=0)

    def dot(x, wgt):
        return jnp.dot(x.astype(BF16), wgt, preferred_element_type=F32)

    def dot_nt(x, wgt):
        return lax.dot_general(x.astype(BF16), wgt, (((1,), (1,)), ((), ())),
                               preferred_element_type=F32)

    def part(z, c, q):
        return z[c * SCAN_CHUNK:(c + 1) * SCAN_CHUNK, q * w:(q + 1) * w]

    cl = _mm_exact_lhs(tri_blk, lw, EXACT_TERMS)
    e_ncl = jnp.exp(-cl)
    a_t = a * jnp.exp(cl - lw)
    r_t = r * jnp.exp(cl)
    b_t = b * e_ncl
    k_t = k * e_ncl
    w_diag = [jnp.concatenate([eye, eye], axis=1)
              * jnp.exp(cl[(c + 1) * SCAN_CHUNK - 1:(c + 1) * SCAN_CHUNK, :]) for c in range(n_chunks)]
    w_diag = jnp.concatenate(w_diag, axis=0)
    w_col = [_mm_exact_rhs(w_diag[:, q * w:(q + 1) * w], seg, EXACT_TERMS)
             for q in range(N_GROUPS)]

    insts = [(c, q) for c in range(n_chunks) for q in range(N_GROUPS)]
    a_i = {i: part(a_t, *i) for i in insts}
    r_i = {i: part(r_t, *i) for i in insts}
    v_i = {i: part(v, *i) for i in insts}
    w_i = {(c, q): w_col[q][c * SCAN_CHUNK:(c + 1) * SCAN_CHUNK, :] for c, q in insts}
    gg = {i: dot_nt(jnp.concatenate([a_i[i], r_i[i], eye], axis=0),
                    jnp.concatenate([bd(part(b_t, *i)), bd(part(k_t, *i))], axis=0))
          for i in insts}
    yield
    l_ab = {i: gg[i][:SCAN_CHUNK, :w] * strict for i in insts}
    l_ak = {i: gg[i][:SCAN_CHUNK, w:] * strict for i in insts}
    m_rb = {i: gg[i][SCAN_CHUNK:2 * SCAN_CHUNK, :w] * incl for i in insts}
    m_rk = {i: gg[i][SCAN_CHUNK:2 * SCAN_CHUNK, w:] * incl for i in insts}
    bh_t = {i: gg[i][2 * SCAN_CHUNK:, :w] * w_i[i] for i in insts}
    kh_t = {i: gg[i][2 * SCAN_CHUNK:, w:] * w_i[i] for i in insts}
    t_inv = {i: eye + l_ab[i] for i in insts}
    pw = {i: dot(l_ab[i], bd(l_ab[i])) for i in insts}
    xv = {i: dot(jnp.concatenate([l_ak[i], m_rk[i], kh_t[i]], axis=0), bd(v_i[i])) for i in insts}
    n = 4
    while n < SCAN_CHUNK:
        yield
        for i in insts:
            both = dot(jnp.concatenate([pw[i], t_inv[i]], axis=0), bd(pw[i]))
            pw[i] = both[:SCAN_CHUNK]
            t_inv[i] = t_inv[i] + both[SCAN_CHUNK:]
        n *= 2
    yield
    for i in insts:
        t_inv[i] = t_inv[i] + dot(t_inv[i], bd(pw[i]))
    yield
    xt = {i: dot(jnp.concatenate([m_rb[i], bh_t[i]], axis=0), bd(t_inv[i])) for i in insts}
    yield
    xpq = {i: dot(xt[i], jnp.concatenate([bd(a_i[i]), bd(xv[i][:SCAN_CHUNK])], axis=1))
           for i in insts}
    yield
    r_phi = {i: jnp.concatenate([r_i[i] + xpq[i][:SCAN_CHUNK, :w], xpq[i][SCAN_CHUNK:, :w]], axis=0)
             for i in insts}
    y_loc = {i: xpq[i][:SCAN_CHUNK, w:] + xv[i][SCAN_CHUNK:2 * SCAN_CHUNK] for i in insts}
    psi_t = {i: xpq[i][SCAN_CHUNK:, w:] + xv[i][2 * SCAN_CHUNK:] for i in insts}
    y_rows = []
    s_t = [st_ref[q] for q in range(N_GROUPS)]
    for c in range(n_chunks):
        y_cols = []
        for q in range(N_GROUPS):
            i = (c, q)
            both = dot(r_phi[i], bd(s_t[q]))
            y_cols.append(both[:SCAN_CHUNK] + y_loc[i])
            s_t[q] = w_i[i] * s_t[q] + both[SCAN_CHUNK:] + psi_t[i]
        y_rows.append(jnp.concatenate(y_cols, axis=1))
        yield
    for q in range(N_GROUPS):
        st_ref[q] = s_t[q]
    return jnp.concatenate(y_rows, axis=0)


def _gmlp_and_gates(h, w_in_ref, lnw_ref, lnb_ref, ws_ref, bst_ref, wbr_gm_ref, ts):
    cw = SIDE_COLS_PER_STEP
    p_cols = []
    for c0 in range(RWKV_COLS, w_in_ref.shape[1], cw):
        p_cols.append(jnp.dot(h, w_in_ref[:, c0:c0 + cw], preferred_element_type=F32))
        yield
    p_side = jnp.concatenate(p_cols, axis=1)
    u_g = _gelu_exact(p_side[:, 0:GMLP_WIDTH])
    yield
    v_g = _gelu_exact(p_side[:, GMLP_WIDTH:2 * GMLP_WIDTH])
    yield
    mu_g = jnp.mean(v_g, axis=-1, keepdims=True)
    vc_g = v_g - mu_g
    var_g = jnp.mean(vc_g * vc_g, axis=-1, keepdims=True)
    v_g = vc_g * lax.rsqrt(var_g + LN_EPS) * lnw_ref[...] + lnb_ref[...]
    yield
    bi = lax.broadcasted_iota(jnp.int32, (GMLP_BLOCK, GMLP_BLOCK), 0)
    bj = lax.broadcasted_iota(jnp.int32, (GMLP_BLOCK, GMLP_BLOCK), 1)
    causal = (bi // STREAM_CHUNK) >= (bj // STREAM_CHUNK)
    gd = GMLP_WIDTH // GMLP_GROUPS
    bst = bst_ref[...]
    sv_cols = []
    for gi in range(GMLP_GROUPS):
        w_g = jnp.where(causal, ws_ref[gi], 0.0)
        bias = bst[:, gi:gi + 1]
        v_blocks = jnp.concatenate(
            [v_g[nb * GMLP_BLOCK:(nb + 1) * GMLP_BLOCK, gi * gd:(gi + 1) * gd]
             for nb in range(ts // GMLP_BLOCK)], axis=1)
        sv = _mm(w_g, v_blocks) + bias
        sv_cols.append(jnp.concatenate(
            [sv[:, nb * gd:(nb + 1) * gd] for nb in range(ts // GMLP_BLOCK)], axis=0))
    y_gm = u_g * jnp.concatenate(sv_cols, axis=1)
    yield
    gate_cols = []
    for c0 in range(2 * GMLP_WIDTH, p_side.shape[1], GMLP_WIDTH):
        gate_cols.append(jax.nn.sigmoid(p_side[:, c0:c0 + GMLP_WIDTH]))
        yield
    gates = jnp.concatenate(gate_cols, axis=1)
    d = gates.shape[1] // 2
    yield
    br_cols = []
    for c0 in range(0, d, MXU_WIDTH):
        br_cols.append(_mm(y_gm, wbr_gm_ref[:, c0:c0 + MXU_WIDTH]))
        yield
    return gates[:, 0:d], gates[:, d:] * jnp.concatenate(br_cols, axis=1)


def _interleave(main, side):
    out = [None, None]
    live = [main, side]
    while any(g is not None for g in live):
        for n, g in enumerate(live):
            if g is not None:
                try:
                    next(g)
                except StopIteration as stop:
                    out[n], live[n] = stop.value, None
    return out


def _mix_kernel(has_vres, ts, *refs):
    it = iter(refs)
    x_ref, mod_ref = next(it), next(it)
    vfirst_ref = next(it) if has_vres else None
    (g1_ref, w_in_ref, mu_ref, w0_ref, w2wa_ref, a0_ref, w2g_ref, kk_ref, ka_ref, rk_ref,
     gnw_ref, gnb_ref) = (next(it) for _ in range(12))
    if has_vres:
        v0_ref, w1r_ref, w2r_ref = next(it), next(it), next(it)
    (lnw_ref, lnb_ref, ws_ref, bst_ref, wbr_rw_ref, wbr_gm_ref, w_out_ref, seg_ref) = (
        next(it) for _ in range(8))
    xo_ref = next(it)
    vo_ref = None if has_vres else next(it)
    st_ref, carry_ref = next(it), next(it)

    d = x_ref.shape[-1]
    j = pl.program_id(1)

    @pl.when(j == 0)
    def _():
        st_ref[...] = jnp.zeros_like(st_ref)
        carry_ref[...] = jnp.zeros_like(carry_ref)

    x = x_ref[...]
    mod = mod_ref[...]
    sh1, sc1, gt1 = mod[:, 0:d], mod[:, d:2 * d], mod[:, 2 * d:3 * d]
    h = (_rms_norm(x, g1_ref[...]) * (1.0 + sc1) + sh1).astype(BF16)

    seg = seg_ref[...]

    def rwkv_branch():
        p_rw = jnp.dot(h, w_in_ref[:, 0:RWKV_COLS], preferred_element_type=F32)
        yield
        row = lax.broadcasted_iota(jnp.int32, (ts, 1), 0)
        prev = jnp.where(row == 0, carry_ref[...], pltpu.roll(p_rw, 1, 0))
        carry_ref[...] = p_rw[ts - 1:ts, :]
        xs = p_rw + (prev - p_rw) * mu_ref[...]
        r = xs[:, 0:RWKV_WIDTH]
        k = xs[:, RWKV_WIDTH:2 * RWKV_WIDTH]
        v = xs[:, 2 * RWKV_WIDTH:3 * RWKV_WIDTH]
        lo = xs[:, COL_LORA:COL_GLORA]
        g_lo = xs[:, COL_GLORA:RWKV_COLS]
        yield
        lane = lax.broadcasted_iota(jnp.int32, (1, LANES), 1)
        lo_act = jnp.where(lane < DECAY_LORA, jnp.tanh(lo), lo)
        wa = _mm(lo_act, w2wa_ref[...])
        w_log = -jax.nn.softplus(-(w0_ref[...] + wa[:, 0:RWKV_WIDTH])) - 0.5
        a = jax.nn.sigmoid(a0_ref[...] + wa[:, RWKV_WIDTH:])
        gate = _mm(jax.nn.sigmoid(g_lo), w2g_ref[...])
        yield
        if has_vres:
            vmix = jax.nn.sigmoid(v0_ref[...] + _mm(_mm(v, w1r_ref[...]), w2r_ref[...]))
            v = v + (vfirst_ref[...] - v) * vmix
        else:
            vo_ref[...] = v
        kk = k * kk_ref[...]
        kk = kk * lax.rsqrt(jnp.maximum(_head_sums(kk * kk, seg), KK_NORM_FLOOR ** 2))
        k = k * (1.0 + (a - 1.0) * ka_ref[...])
        bonus = _head_sums(r * k * rk_ref[...], seg) * v
        yield
        y = yield from _scan_tile(r, -jnp.exp(w_log), k, v, -kk, kk * a, seg, st_ref)
        return y, bonus, gate

    (y, bonus, gate), (gate_rw, gm_term) = _interleave(
        rwkv_branch(),
        _gmlp_and_gates(h, w_in_ref, lnw_ref, lnb_ref, ws_ref, bst_ref, wbr_gm_ref, ts))
    mean = _head_sums(y, seg) * (1.0 / HEAD_DIM)
    yc = y - mean
    var = _head_sums(yc * yc, seg) * (1.0 / HEAD_DIM)
    y = yc * lax.rsqrt(var + GN_EPS) * gnw_ref[...] + gnb_ref[...]
    y_rw = (y + bonus) * gate

    merged = gate_rw * _mm(y_rw, wbr_rw_ref[...]) + gm_term
    xo_ref[...] = x + gt1 * _mm(merged, w_out_ref[...])


def _const_spec(shape, layer=None):
    if layer is None:
        return pl.BlockSpec(shape, lambda b, j: (0,) * len(shape), pipeline_mode=pl.Buffered(1))
    return pl.BlockSpec((None,) + shape, lambda b, j: (layer,) + (0,) * len(shape),
                        pipeline_mode=pl.Buffered(1))


def _token_mix(layer, x, mod, v_first, params, ts):
    batch, seq, d = x.shape
    has_vres = layer > 0
    tile = lambda w: pl.BlockSpec((None, ts, w), lambda b, j: (b, j, 0))
    row512 = _const_spec((1, RWKV_WIDTH), layer)
    p = params
    operands = [x, mod]
    in_specs = [tile(d), pl.BlockSpec((None, 1, N_MOD * d), lambda b, j: (b, 0, 0))]
    if has_vres:
        operands.append(v_first)
        in_specs.append(tile(RWKV_WIDTH))
    operands += [p["norm1_g"], p["w_in"], p["mu_shift"], p["w0_decay"], p["w2_wa"], p["a0"],
                 p["w2_gate"], p["k_k"], p["k_a"], p["r_k"], p["gn_w"], p["gn_b"]]
    in_specs += [_const_spec((1, d), layer), _const_spec(p["w_in"].shape[1:], layer),
                 _const_spec((1, RWKV_COLS), layer), row512,
                 _const_spec(p["w2_wa"].shape[1:], layer), row512,
                 _const_spec(p["w2_gate"].shape[1:], layer), row512, row512, row512, row512, row512]
    if has_vres:
        operands += [p["v0_res"], p["w1_res"], p["w2_res"]]
        in_specs += [_const_spec((1, RWKV_WIDTH), layer - 1),
                     _const_spec(p["w1_res"].shape[1:], layer - 1),
                     _const_spec(p["w2_res"].shape[1:], layer - 1)]
    operands += [p["ln_gmlp_w"], p["ln_gmlp_b"], p["w_spatial"], p["b_spatial_t"],
                 p["w_br_rwkv"], p["w_br_gmlp"], p["w_out"], p["seg_ones"]]
    in_specs += [_const_spec((1, GMLP_WIDTH), layer), _const_spec((1, GMLP_WIDTH), layer),
                 _const_spec(p["w_spatial"].shape[1:], layer),
                 _const_spec(p["b_spatial_t"].shape[1:], layer),
                 _const_spec(p["w_br_rwkv"].shape[1:], layer),
                 _const_spec(p["w_br_gmlp"].shape[1:], layer),
                 _const_spec(p["w_out"].shape[1:], layer),
                 _const_spec(p["seg_ones"].shape)]
    out_shape = [jax.ShapeDtypeStruct((batch, seq, d), F32)]
    out_specs = [tile(d)]
    if not has_vres:
        out_shape.append(jax.ShapeDtypeStruct((batch, seq, RWKV_WIDTH), F32))
        out_specs.append(tile(RWKV_WIDTH))
    scratch = [pltpu.VMEM((N_GROUPS, HEAD_DIM, MXU_WIDTH), F32), pltpu.VMEM((1, RWKV_COLS), F32)]
    outs = pl.pallas_call(
        functools.partial(_mix_kernel, has_vres, ts),
        grid=(batch, seq // ts),
        in_specs=in_specs,
        out_specs=out_specs,
        out_shape=out_shape,
        scratch_shapes=scratch,
        compiler_params=pltpu.CompilerParams(
            dimension_semantics=("arbitrary", "arbitrary"),
            vmem_limit_bytes=VMEM_LIMIT_BYTES),
    )(*operands)
    return outs if has_vres else (outs[0], outs[1])


def _ffn_kernel(final, ff_chunk, *refs):
    if final:
        x_ref, mod_ref, g2_ref, w1_ref, w2_ref, fg_ref, o_ref = refs
    else:
        x_ref, mod_ref, g2_ref, w1_ref, w2_ref, o_ref = refs
    d = x_ref.shape[-1]
    x = x_ref[...]
    mod = mod_ref[...]
    sh2, sc2, gt2 = mod[:, 3 * d:4 * d], mod[:, 4 * d:5 * d], mod[:, 5 * d:6 * d]
    h = (_rms_norm(x, g2_ref[...]) * (1.0 + sc2) + sh2).astype(BF16)
    acc = jnp.zeros_like(x)
    for c0 in range(0, w1_ref.shape[1], ff_chunk):
        t = jnp.maximum(jnp.dot(h, w1_ref[:, c0:c0 + ff_chunk], preferred_element_type=F32), 0.0)
        acc = acc + _mm(t * t, w2_ref[c0:c0 + ff_chunk, :])
    xn = x + gt2 * acc
    if final:
        xn = _rms_norm(xn, fg_ref[...])
    o_ref[...] = xn


def _channel_mix(layer, x, mod, params, final, tm, ff_chunk=FFN_CHUNK):
    batch, seq, d = x.shape
    tile = pl.BlockSpec((None, tm, d), lambda b, j: (b, j, 0))
    operands = [x, mod, params["norm2_g"], params["w_ff1"], params["w_ff2"]]
    in_specs = [tile, pl.BlockSpec((None, 1, N_MOD * d), lambda b, j: (b, 0, 0)),
                _const_spec((1, d), layer), _const_spec(params["w_ff1"].shape[1:], layer),
                _const_spec(params["w_ff2"].shape[1:], layer)]
    if final:
        operands.append(params["final_g"])
        in_specs.append(_const_spec((1, d)))
    return pl.pallas_call(
        functools.partial(_ffn_kernel, final, ff_chunk),
        grid=(batch, seq // tm),
        in_specs=in_specs,
        out_specs=tile,
        out_shape=jax.ShapeDtypeStruct((batch, seq, d), F32),
        compiler_params=pltpu.CompilerParams(
            dimension_semantics=("arbitrary", "arbitrary"),
            vmem_limit_bytes=VMEM_LIMIT_BYTES),
    )(*operands)


def kernel(x, c, w_ada, b_ada, norm1_g, norm2_g, w_in, mu_shift, w0_decay, w2_decay, a0, w2_aaa, w2_gate, k_k, k_a, r_k, gn_w, gn_b, v0_res, w1_res, w2_res, ln_gmlp_w, ln_gmlp_b, w_spatial, b_spatial, w_br_rwkv, w_br_gmlp, w_out, w_ff1, w_ff2, final_g):
    depth = w_in.shape[0]
    batch, seq, d = x.shape
    ts = min(MIX_TOKENS, seq)
    tm = min(FFN_TOKENS, seq)
    row = lambda t: t.reshape(t.shape[0], 1, -1)
    zeros = lambda *s: jnp.zeros(s, F32)
    w2_wa = jnp.concatenate([
        jnp.concatenate([w2_decay, zeros(depth, DECAY_LORA, RWKV_WIDTH)], axis=2),
        jnp.concatenate([zeros(depth, AAA_LORA, RWKV_WIDTH), w2_aaa], axis=2)], axis=1)
    pad = LANES - VRES_LORA
    head_id = jnp.arange(MXU_WIDTH) // HEAD_DIM
    params = dict(
        norm1_g=row(norm1_g), norm2_g=row(norm2_g), w_in=w_in.astype(BF16), mu_shift=row(mu_shift),
        w0_decay=row(w0_decay), w2_wa=w2_wa.astype(BF16), a0=row(a0), w2_gate=w2_gate.astype(BF16),
        k_k=row(k_k), k_a=row(k_a), r_k=row(r_k), gn_w=row(gn_w), gn_b=row(gn_b),
        v0_res=row(v0_res),
        w1_res=jnp.pad(w1_res, ((0, 0), (0, 0), (0, pad))).astype(BF16),
        w2_res=jnp.pad(w2_res, ((0, 0), (0, pad), (0, 0))).astype(BF16),
        ln_gmlp_w=row(ln_gmlp_w), ln_gmlp_b=row(ln_gmlp_b), w_spatial=w_spatial,
        b_spatial_t=jnp.swapaxes(b_spatial, 1, 2),
        w_br_rwkv=w_br_rwkv.astype(BF16), w_br_gmlp=w_br_gmlp.astype(BF16),
        w_out=w_out.astype(BF16), w_ff1=w_ff1.astype(BF16), w_ff2=w_ff2.astype(BF16),
        final_g=final_g.reshape(1, d),
        seg_ones=(head_id[:, None] == head_id[None, :]).astype(BF16),
    )
    mod = _modulation(c, w_ada, b_ada)
    v_first = None
    for layer in range(depth):
        mod_l = mod[layer].reshape(batch, 1, N_MOD * d)
        if layer == 0:
            x, v_first = _token_mix(layer, x, mod_l, None, params, ts)
        else:
            (x,) = _token_mix(layer, x, mod_l, v_first, params, ts)
        x = _channel_mix(layer, x, mod_l, params, layer == depth - 1, tm)
    return x
```

```python
import functools

import jax
import jax.numpy as jnp
from jax import lax
from jax.experimental import pallas as pl
from jax.experimental.pallas import tpu as pltpu

F32 = jnp.float32
BF16 = jnp.bfloat16

RWKV_HEADS = 8
HEAD_DIM = 64
RWKV_WIDTH = RWKV_HEADS * HEAD_DIM
DECAY_LORA = 64
AAA_LORA = 64
GATE_LORA = 128
VRES_LORA = 32
GMLP_BLOCK = 128
GMLP_GROUPS = 4
GMLP_WIDTH = 512
STREAM_CHUNK = 64
N_MOD = 6
RMS_EPS = 1e-6
LN_EPS = 1e-5
GN_EPS = 64e-5
KK_NORM_FLOOR = 1e-12
EXACT_TERMS = 2
MIX_TOKENS = 256
FFN_CHUNK = 1024
FUSED_FFN_CHUNK = 512

LANES = 128
MXU_WIDTH = 256
HEADS_PER_GROUP = MXU_WIDTH // HEAD_DIM
N_GROUPS = RWKV_HEADS // HEADS_PER_GROUP
SCAN_CHUNK = 64
SIDE_COLS_PER_STEP = 3 * MXU_WIDTH
VMEM_LIMIT_BYTES =56 * 1024 * 1024

COL_LORA = 3 * RWKV_WIDTH
COL_GLORA = COL_LORA + DECAY_LORA + AAA_LORA
RWKV_COLS = COL_GLORA + GATE_LORA
COL_GATES = RWKV_COLS + 2 * GMLP_WIDTH


def _mm(a, b):
    return jnp.dot(a.astype(BF16), b.astype(BF16), preferred_element_type=F32)


def _split_bf16(x, terms):
    parts = []
    for _ in range(terms):
        p = x.astype(BF16)
        parts.append(p)
        x = x - p.astype(F32)
    return parts


def _head_sums(x, seg):
    w = seg.shape[0]
    xb = x.astype(BF16)
    return jnp.concatenate(
        [jnp.dot(xb[:, c0:c0 + w], seg, preferred_element_type=F32)
         for c0 in range(0, x.shape[1], w)], axis=1)


def _mm_exact_rhs(a, m01, terms):
    out = None
    for p in _split_bf16(a, terms):
        t = jnp.dot(p, m01, preferred_element_type=F32)
        out = t if out is None else out + t
    return out


def _mm_exact_lhs(m01, b, terms):
    out = None
    for p in _split_bf16(b, terms):
        t = jnp.dot(m01, p, preferred_element_type=F32)
        out = t if out is None else out + t
    return out


def _gelu_exact(x):
    return 0.5 * x * (1.0 + lax.erf(x * (2.0 ** -0.5)))


def _rms_norm(x, gain):
    ms = jnp.mean(x * x, axis=-1, keepdims=True)
    return (x * lax.rsqrt(ms + RMS_EPS)) * gain


def _mod_kernel(c_ref, w_ref, b_ref, o_ref):
    c = c_ref[...]
    c_act = c * jax.nn.sigmoid(c)
    o_ref[...] = _mm(c_act, w_ref[...]) + b_ref[...]


def _modulation(c, w_ada, b_ada):
    depth, d, _ = w_ada.shape
    batch = c.shape[0]
    return pl.pallas_call(
        _mod_kernel,
        grid=(depth, N_MOD),
        in_specs=[
            pl.BlockSpec((batch, d), lambda l, j: (0, 0)),
            pl.BlockSpec((None, d, d), lambda l, j: (l, 0, j)),
            pl.BlockSpec((None, 1, d), lambda l, j: (l, 0, j)),
        ],
        out_specs=pl.BlockSpec((None, batch, d), lambda l, j: (l, 0, j)),
        out_shape=jax.ShapeDtypeStruct((depth, batch, N_MOD * d), F32),
        compiler_params=pltpu.CompilerParams(
            dimension_semantics=("arbitrary", "arbitrary"),
            vmem_limit_bytes=VMEM_LIMIT_BYTES),
    )(c, w_ada, b_ada.reshape(depth, 1, N_MOD * d))


def _scan_tile(r, lw, k, v, a, b, seg, st_ref):
    ts = r.shape[0]
    n_chunks = ts // SCAN_CHUNK
    w = MXU_WIDTH
    ti = lax.broadcasted_iota(jnp.int32, (ts, ts), 0)
    tj = lax.broadcasted_iota(jnp.int32, (ts, ts), 1)
    tri_blk = ((ti // SCAN_CHUNK == tj // SCAN_CHUNK) & (ti >= tj)).astype(BF16)
    t_row = lax.broadcasted_iota(jnp.int32, (SCAN_CHUNK, MXU_WIDTH), 0)
    g_lane = lax.broadcasted_iota(jnp.int32, (SCAN_CHUNK, MXU_WIDTH), 1)
    t_col = g_lane % HEAD_DIM
    strict = lambda z: jnp.where(t_row > t_col, z, 0.0)
    incl = lambda z: jnp.where(t_row >= t_col, z, 0.0)
    eye = (t_row == t_col).astype(F32)
    head_of_lane = g_lane[0:1, :] // HEAD_DIM
    head_masks = [head_of_lane == hh for hh in range(HEADS_PER_GROUP)]

    def bd(x):
        xb = x.astype(BF16)
        return jnp.concatenate([jnp.where(m, xb, jnp.zeros_like(xb)) for m in head_masks], axis=0)

    def dot(x, wgt):
        return jnp.dot(x.astype(BF16), wgt, preferred_element_type=F32)

    def dot_nt(x, wgt):
        return lax.dot_general(x.astype(BF16), wgt, (((1,), (1,)), ((), ())),
                               preferred_element_type=F32)

    def part(z, c, q):
        return z[c * SCAN_CHUNK:(c + 1) * SCAN_CHUNK, q * w:(q + 1) * w]

    cl = _mm_exact_lhs(tri_blk, lw, EXACT_TERMS)
    e_ncl = jnp.exp(-cl)
    a_t = a * jnp.exp(cl - lw)
    r_t = r * jnp.exp(cl)
    b_t = b * e_ncl
    k_t = k * e_ncl
    w_diag = [jnp.concatenate([eye, eye], axis=1)
              * jnp.exp(cl[(c + 1) * SCAN_CHUNK - 1:(c + 1) * SCAN_CHUNK, :]) for c in range(n_chunks)]
    w_diag = jnp.concatenate(w_diag, axis=0)
    w_col = [_mm_exact_rhs(w_diag[:, q * w:(q + 1) * w], seg, EXACT_TERMS)
             for q in range(N_GROUPS)]

    insts = [(c, q) for c in range(n_chunks) for q in range(N_GROUPS)]
    a_i = {i: part(a_t, *i) for i in insts}
    r_i = {i: part(r_t, *i) for i in insts}
    v_i = {i: part(v, *i) for i in insts}
    w_i = {(c, q): w_col[q][c * SCAN_CHUNK:(c + 1) * SCAN_CHUNK, :] for c, q in insts}
    gg = {i: dot_nt(jnp.concatenate([a_i[i], r_i[i], eye], axis=0),
                    jnp.concatenate([bd(part(b_t, *i)), bd(part(k_t, *i))], axis=0))
          for i in insts}
    yield
    l_ab = {i: strict(gg[i][:SCAN_CHUNK, :w]) for i in insts}
    l_ak = {i: strict(gg[i][:SCAN_CHUNK, w:]) for i in insts}
    m_rb = {i: incl(gg[i][SCAN_CHUNK:2 * SCAN_CHUNK, :w]) for i in insts}
    m_rk = {i: incl(gg[i][SCAN_CHUNK:2 * SCAN_CHUNK, w:]) for i in insts}
    bh_t = {i: gg[i][2 * SCAN_CHUNK:, :w] * w_i[i] for i in insts}
    kh_t = {i: gg[i][2 * SCAN_CHUNK:, w:] * w_i[i] for i in insts}
    t_inv = {i: eye + l_ab[i] for i in insts}
    pw = {i: dot(l_ab[i], bd(l_ab[i])) for i in insts}
    xv = {i: dot(jnp.concatenate([l_ak[i], m_rk[i], kh_t[i]], axis=0), bd(v_i[i])) for i in insts}
    n = 4
    while n < SCAN_CHUNK:
        yield
        for i in insts:
            both = dot(jnp.concatenate([pw[i], t_inv[i]], axis=0), bd(pw[i]))
            pw[i] = both[:SCAN_CHUNK]
            t_inv[i] = t_inv[i] + both[SCAN_CHUNK:]
        n *= 2
    yield
    for i in insts:
        t_inv[i] = t_inv[i] + dot(t_inv[i], bd(pw[i]))
    yield
    xt = {i: dot(jnp.concatenate([m_rb[i], bh_t[i]], axis=0), bd(t_inv[i])) for i in insts}
    yield
    xpq = {i: dot(xt[i], jnp.concatenate([bd(a_i[i]), bd(xv[i][:SCAN_CHUNK])], axis=1))
           for i in insts}
    yield
    r_phi = {i: jnp.concatenate([r_i[i] + xpq[i][:SCAN_CHUNK, :w], xpq[i][SCAN_CHUNK:, :w]], axis=0)
             for i in insts}
    y_loc = {i: xpq[i][:SCAN_CHUNK, w:] + xv[i][SCAN_CHUNK:2 * SCAN_CHUNK] for i in insts}
    psi_t = {i: xpq[i][SCAN_CHUNK:, w:] + xv[i][2 * SCAN_CHUNK:] for i in insts}
    y_rows = []
    s_t = [st_ref[q] for q in range(N_GROUPS)]
    for c in range(n_chunks):
        y_cols = []
        for q in range(N_GROUPS):
            i = (c, q)
            both = dot(r_phi[i], bd(s_t[q]))
            y_cols.append(both[:SCAN_CHUNK] + y_loc[i])
            s_t[q] = w_i[i] * s_t[q] + both[SCAN_CHUNK:] + psi_t[i]
        y_rows.append(jnp.concatenate(y_cols, axis=1))
        yield
    for q in range(N_GROUPS):
        st_ref[q] = s_t[q]
    return jnp.concatenate(y_rows, axis=0)


def _gmlp_and_gates(h, w_in_ref, lnw_ref, lnb_ref, ws_ref, bst_ref, wbr_gm_ref, ts):
    cw = SIDE_COLS_PER_STEP
    p_cols = []
    for c0 in range(RWKV_COLS, w_in_ref.shape[1], cw):
        p_cols.append(jnp.dot(h, w_in_ref[:, c0:c0 + cw], preferred_element_type=F32))
        yield
    p_side = jnp.concatenate(p_cols, axis=1)
    u_g = _gelu_exact(p_side[:, 0:GMLP_WIDTH])
    yield
    v_g = _gelu_exact(p_side[:, GMLP_WIDTH:2 * GMLP_WIDTH])
    yield
    mu_g = jnp.mean(v_g, axis=-1, keepdims=True)
    vc_g = v_g - mu_g
    var_g = jnp.mean(vc_g * vc_g, axis=-1, keepdims=True)
    v_g = vc_g * lax.rsqrt(var_g + LN_EPS) * lnw_ref[...] + lnb_ref[...]
    yield
    bi = lax.broadcasted_iota(jnp.int32, (GMLP_BLOCK, GMLP_BLOCK), 0)
    bj = lax.broadcasted_iota(jnp.int32, (GMLP_BLOCK, GMLP_BLOCK), 1)
    causal = (bi // STREAM_CHUNK) >= (bj // STREAM_CHUNK)
    gd = GMLP_WIDTH // GMLP_GROUPS
    bst = bst_ref[...]
    sv_cols = []
    for gi in range(GMLP_GROUPS):
        w_g = jnp.where(causal, ws_ref[gi], 0.0)
        bias = bst[:, gi:gi + 1]
        v_blocks = jnp.concatenate(
            [v_g[nb * GMLP_BLOCK:(nb + 1) * GMLP_BLOCK, gi * gd:(gi + 1) * gd]
             for nb in range(ts // GMLP_BLOCK)], axis=1)
        sv = _mm(w_g, v_blocks) + bias
        sv_cols.append(jnp.concatenate(
            [sv[:, nb * gd:(nb + 1) * gd] for nb in range(ts // GMLP_BLOCK)], axis=0))
    y_gm = u_g * jnp.concatenate(sv_cols, axis=1)
    yield
    gate_cols = []
    for c0 in range(2 * GMLP_WIDTH, p_side.shape[1], GMLP_WIDTH):
        gate_cols.append(jax.nn.sigmoid(p_side[:, c0:c0 + GMLP_WIDTH]))
        yield
    gates = jnp.concatenate(gate_cols, axis=1)
    d = gates.shape[1] // 2
    yield
    br_cols = []
    for c0 in range(0, d, MXU_WIDTH):
        br_cols.append(_mm(y_gm, wbr_gm_ref[:, c0:c0 + MXU_WIDTH]))
        yield
    return gates[:, 0:d], gates[:, d:] * jnp.concatenate(br_cols, axis=1)


def _interleave(*gens):
    out = [None] * len(gens)
    live = list(gens)
    while any(g is not None for g in live):
        for n, g in enumerate(live):
            if g is not None:
                try:
                    next(g)
                except StopIteration as stop:
                    out[n], live[n] = stop.value, None
    return out


def _ffn_steps(x, mod, g2_ref, w1_ref, w2_ref, fg_ref, chunk):
    d = x.shape[-1]
    sh2, sc2, gt2 = mod[:, 3 * d:4 * d], mod[:, 4 * d:5 * d], mod[:, 5 * d:6 * d]
    h = (_rms_norm(x, g2_ref[...]) * (1.0 + sc2) + sh2).astype(BF16)
    yield
    acc = None
    for c0 in range(0, w1_ref.shape[1], chunk):
        t = jnp.maximum(jnp.dot(h, w1_ref[:, c0:c0 + chunk], preferred_element_type=F32), 0.0)
        part = _mm(t * t, w2_ref[c0:c0 + chunk, :])
        acc = part if acc is None else acc + part
        yield
    xn = x + gt2 * acc
    if fg_ref is not None:
        xn = _rms_norm(xn, fg_ref[...])
    return xn


def _layer_kernel(has_vres, final, ts, tiles_per_row, *refs):
    it = iter(refs)
    x_ref, mod_ref, modp_ref = next(it), next(it), next(it)
    vfirst_ref = next(it) if has_vres else None
    (g1_ref, w_in_ref, mu_ref, w0_ref, w2wa_ref, a0_ref, w2g_ref, kk_ref, ka_ref, rk_ref,
     gnw_ref, gnb_ref) = (next(it) for _ in range(12))
    if has_vres:
        v0_ref, w1r_ref, w2r_ref = next(it), next(it), next(it)
    (lnw_ref, lnb_ref, ws_ref, bst_ref, wbr_rw_ref, wbr_gm_ref, w_out_ref, seg_ref) = (
        next(it) for _ in range(8))
    g2_ref, w1_ref, w2_ref = next(it), next(it), next(it)
    fg_ref = next(it) if final else None
    xo_ref, xlast_ref = next(it), next(it)
    vo_ref = None if has_vres else next(it)
    st_ref, carry_ref, xmid_ref = next(it), next(it), next(it)

    d = x_ref.shape[-1]
    s = pl.program_id(0)

    @pl.when(s % tiles_per_row == 0)
    def _():
        st_ref[...] = jnp.zeros_like(st_ref)
        carry_ref[...] = jnp.zeros_like(carry_ref)

    @pl.when(s == 0)
    def _():
        xmid_ref[...] = jnp.zeros_like(xmid_ref)

    x_prev = xmid_ref[...]
    x = x_ref[...]
    mod = mod_ref[...]
    sh1, sc1, gt1 = mod[:, 0:d], mod[:, d:2 * d], mod[:, 2 * d:3 * d]
    h = (_rms_norm(x, g1_ref[...]) * (1.0 + sc1) + sh1).astype(BF16)

    seg = seg_ref[...]

    def rwkv_branch():
        p_rw = jnp.dot(h, w_in_ref[:, 0:RWKV_COLS], preferred_element_type=F32)
        yield
        row = lax.broadcasted_iota(jnp.int32, (ts, 1), 0)
        prev = jnp.where(row == 0, carry_ref[...], pltpu.roll(p_rw, 1, 0))
        carry_ref[...] = p_rw[ts - 1:ts, :]
        xs = p_rw + (prev - p_rw) * mu_ref[...]
        r = xs[:, 0:RWKV_WIDTH]
        k = xs[:, RWKV_WIDTH:2 * RWKV_WIDTH]
        v = xs[:, 2 * RWKV_WIDTH:3 * RWKV_WIDTH]
        lo = xs[:, COL_LORA:COL_GLORA]
        g_lo = xs[:, COL_GLORA:RWKV_COLS]
        yield
        lane = lax.broadcasted_iota(jnp.int32, (1, LANES), 1)
        lo_act = jnp.where(lane < DECAY_LORA, jnp.tanh(lo), lo)
        wa = _mm(lo_act, w2wa_ref[...])
        w_log = -jax.nn.softplus(-(w0_ref[...] + wa[:, 0:RWKV_WIDTH])) - 0.5
        a = jax.nn.sigmoid(a0_ref[...] + wa[:, RWKV_WIDTH:])
        gate = _mm(jax.nn.sigmoid(g_lo), w2g_ref[...])
        yield
        if has_vres:
            vmix = jax.nn.sigmoid(v0_ref[...] + _mm(_mm(v, w1r_ref[...]), w2r_ref[...]))
            v = v + (vfirst_ref[...] - v) * vmix
        else:
            vo_ref[...] = v
        kk = k * kk_ref[...]
        kk = kk * lax.rsqrt(jnp.maximum(_head_sums(kk * kk, seg), KK_NORM_FLOOR ** 2))
        k = k * (1.0 + (a - 1.0) * ka_ref[...])
        bonus = _head_sums(r * k * rk_ref[...], seg) * v
        yield
        y = yield from _scan_tile(r, -jnp.exp(w_log), k, v, -kk, kk * a, seg, st_ref)
        return y, bonus, gate

    (y, bonus, gate), (gate_rw, gm_term), x_prev_out = _interleave(
        rwkv_branch(),
        _gmlp_and_gates(h, w_in_ref, lnw_ref, lnb_ref, ws_ref, bst_ref, wbr_gm_ref, ts),
        _ffn_steps(x_prev, modp_ref[...], g2_ref, w1_ref, w2_ref, fg_ref, FUSED_FFN_CHUNK))
    xo_ref[...] = x_prev_out
    mean = _head_sums(y, seg) * (1.0 / HEAD_DIM)
    yc = y - mean
    var = _head_sums(yc * yc, seg) * (1.0 / HEAD_DIM)
    y = yc * lax.rsqrt(var + GN_EPS) * gnw_ref[...] + gnb_ref[...]
    y_rw = (y + bonus) * gate

    merged = gate_rw * _mm(y_rw, wbr_rw_ref[...]) + gm_term
    x_mixed = x + gt1 * _mm(merged, w_out_ref[...])
    xmid_ref[...] = x_mixed
    xlast_ref[...] = x_mixed


def _const_spec(shape, layer=None):
    if layer is None:
        return pl.BlockSpec(shape, lambda *_: (0,) * len(shape), pipeline_mode=pl.Buffered(1))
    return pl.BlockSpec((None,) + shape, lambda *_: (layer,) + (0,) * len(shape),
                        pipeline_mode=pl.Buffered(1))


def _layer(layer, x, mod, v_first, params, final, ts):
    batch, seq, d = x.shape
    has_vres = layer > 0
    tpr = seq // ts
    cur = lambda w: pl.BlockSpec((None, ts, w), lambda s: (s // tpr, s % tpr, 0))
    prev_tile = lambda s: jnp.maximum(s - 1, 0)
    prev = pl.BlockSpec((None, ts, d), lambda s: (prev_tile(s) // tpr, prev_tile(s) % tpr, 0))
    row512 = _const_spec((1, RWKV_WIDTH), layer)
    p = params
    operands = [x, mod, mod]
    in_specs = [cur(d), pl.BlockSpec((None, 1, N_MOD * d), lambda s: (s // tpr, 0, 0)),
                pl.BlockSpec((None, 1, N_MOD * d), lambda s: (prev_tile(s) // tpr, 0, 0))]
    if has_vres:
        operands.append(v_first)
        in_specs.append(cur(RWKV_WIDTH))
    operands += [p["norm1_g"], p["w_in"], p["mu_shift"], p["w0_decay"], p["w2_wa"], p["a0"],
                 p["w2_gate"], p["k_k"], p["k_a"], p["r_k"], p["gn_w"], p["gn_b"]]
    in_specs += [_const_spec((1, d), layer), _const_spec(p["w_in"].shape[1:], layer),
                 _const_spec((1, RWKV_COLS), layer), row512,
                 _const_spec(p["w2_wa"].shape[1:], layer), row512,
                 _const_spec(p["w2_gate"].shape[1:], layer), row512, row512, row512, row512, row512]
    if has_vres:
        operands += [p["v0_res"], p["w1_res"], p["w2_res"]]
        in_specs += [_const_spec((1, RWKV_WIDTH), layer - 1),
                     _const_spec(p["w1_res"].shape[1:], layer - 1),
                     _const_spec(p["w2_res"].shape[1:], layer - 1)]
    operands += [p["ln_gmlp_w"], p["ln_gmlp_b"], p["w_spatial"], p["b_spatial_t"],
                 p["w_br_rwkv"], p["w_br_gmlp"], p["w_out"], p["seg_ones"]]
    in_specs += [_const_spec((1, GMLP_WIDTH), layer), _const_spec((1, GMLP_WIDTH), layer),
                 _const_spec(p["w_spatial"].shape[1:], layer),
                 _const_spec(p["b_spatial_t"].shape[1:], layer),
                 _const_spec(p["w_br_rwkv"].shape[1:], layer),
                 _const_spec(p["w_br_gmlp"].shape[1:], layer),
                 _const_spec(p["w_out"].shape[1:], layer),
                 _const_spec(p["seg_ones"].shape)]
    operands += [p["norm2_g"], p["w_ff1"], p["w_ff2"]]
    in_specs += [_const_spec((1, d), layer), _const_spec(p["w_ff1"].shape[1:], layer),
                 _const_spec(p["w_ff2"].shape[1:], layer)]
    if final:
        operands.append(p["final_g"])
        in_specs.append(_const_spec((1, d)))
    out_shape = [jax.ShapeDtypeStruct((batch, seq, d), F32), jax.ShapeDtypeStruct((ts, d), F32)]
    out_specs = [prev, pl.BlockSpec((ts, d), lambda s: (0, 0))]
    if not has_vres:
        out_shape.append(jax.ShapeDtypeStruct((batch, seq, RWKV_WIDTH), F32))
        out_specs.append(cur(RWKV_WIDTH))
    scratch = [pltpu.VMEM((N_GROUPS, HEAD_DIM, MXU_WIDTH), F32), pltpu.VMEM((1, RWKV_COLS), F32),
               pltpu.VMEM((ts, d), F32)]
    return pl.pallas_call(
        functools.partial(_layer_kernel, has_vres, final, ts, tpr),
        grid=(batch * tpr,),
        in_specs=in_specs,
        out_specs=out_specs,
        out_shape=out_shape,
        scratch_shapes=scratch,
        compiler_params=pltpu.CompilerParams(
            dimension_semantics=("arbitrary",),
            vmem_limit_bytes=VMEM_LIMIT_BYTES),
    )(*operands)


def _ffn_tail_kernel(final, *refs):
    if final:
        x_ref, mod_ref, g2_ref, w1_ref, w2_ref, fg_ref, _, o_ref = refs
    else:
        x_ref, mod_ref, g2_ref, w1_ref, w2_ref, _, o_ref = refs
        fg_ref = None
    (o_ref[...],) = _interleave(
        _ffn_steps(x_ref[...], mod_ref[...], g2_ref, w1_ref, w2_ref, fg_ref, FFN_CHUNK))


def _ffn_tail(layer, x_out, x_last, mod, params, final):
    batch, seq, d = x_out.shape
    ts = x_last.shape[0]
    one = lambda shape, *idx: pl.BlockSpec(shape, lambda s: idx)
    operands = [x_last, mod, params["norm2_g"], params["w_ff1"], params["w_ff2"]]
    in_specs = [one((ts, d), 0, 0), one((None, 1, N_MOD * d), batch - 1, 0, 0),
                one((None, 1, d), layer, 0, 0),
                one((None,) + params["w_ff1"].shape[1:], layer, 0, 0),
                one((None,) + params["w_ff2"].shape[1:], layer, 0, 0)]
    if final:
        operands.append(params["final_g"])
        in_specs.append(one((1, d), 0, 0))
    operands.append(x_out)
    in_specs.append(pl.BlockSpec(memory_space=pl.ANY))
    return pl.pallas_call(
        functools.partial(_ffn_tail_kernel, final),
        grid=(1,),
        in_specs=in_specs,
        out_specs=one((None, ts, d), batch - 1, seq // ts - 1, 0),
        out_shape=jax.ShapeDtypeStruct((batch, seq, d), F32),
        input_output_aliases={len(operands) - 1: 0},
        compiler_params=pltpu.CompilerParams(
            dimension_semantics=("arbitrary",),
            vmem_limit_bytes=VMEM_LIMIT_BYTES),
    )(*operands)


def kernel(x, c, w_ada, b_ada, norm1_g, norm2_g, w_in, mu_shift, w0_decay, w2_decay, a0, w2_aaa, w2_gate, k_k, k_a, r_k, gn_w, gn_b, v0_res, w1_res, w2_res, ln_gmlp_w, ln_gmlp_b, w_spatial, b_spatial, w_br_rwkv, w_br_gmlp, w_out, w_ff1, w_ff2, final_g):
    depth = w_in.shape[0]
    batch, seq, d = x.shape
    ts = min(MIX_TOKENS, seq)
    row = lambda t: t.reshape(t.shape[0], 1, -1)
    zeros = lambda *s: jnp.zeros(s, F32)
    w2_wa = jnp.concatenate([
        jnp.concatenate([w2_decay, zeros(depth, DECAY_LORA, RWKV_WIDTH)], axis=2),
        jnp.concatenate([zeros(depth, AAA_LORA, RWKV_WIDTH), w2_aaa], axis=2)], axis=1)
    pad = LANES - VRES_LORA
    head_id = jnp.arange(MXU_WIDTH) // HEAD_DIM
    params = dict(
        norm1_g=row(norm1_g), norm2_g=row(norm2_g), w_in=w_in.astype(BF16), mu_shift=row(mu_shift),
        w0_decay=row(w0_decay), w2_wa=w2_wa.astype(BF16), a0=row(a0), w2_gate=w2_gate.astype(BF16),
        k_k=row(k_k), k_a=row(k_a), r_k=row(r_k), gn_w=row(gn_w), gn_b=row(gn_b),
        v0_res=row(v0_res),
        w1_res=jnp.pad(w1_res, ((0, 0), (0, 0), (0, pad))).astype(BF16),
        w2_res=jnp.pad(w2_res, ((0, 0), (0, pad), (0, 0))).astype(BF16),
        ln_gmlp_w=row(ln_gmlp_w), ln_gmlp_b=row(ln_gmlp_b), w_spatial=w_spatial,
        b_spatial_t=jnp.swapaxes(b_spatial, 1, 2),
        w_br_rwkv=w_br_rwkv.astype(BF16), w_br_gmlp=w_br_gmlp.astype(BF16),
        w_out=w_out.astype(BF16), w_ff1=w_ff1.astype(BF16), w_ff2=w_ff2.astype(BF16),
        final_g=final_g.reshape(1, d),
        seg_ones=(head_id[:, None] == head_id[None, :]).astype(BF16),
    )
    mod = _modulation(c, w_ada, b_ada)
    v_first = None
    for layer in range(depth):
        mod_l = mod[layer].reshape(batch, 1, N_MOD * d)
        final = layer == depth - 1
        outs = _layer(layer, x, mod_l, v_first, params, final, ts)
        if layer == 0:
            v_first = outs[2]
        x = _ffn_tail(layer, outs[0], outs[1], mod_l, params, final)
    return x
```

```python
import functools

import jax
import jax.numpy as jnp
from jax import lax
from jax.experimental import pallas as pl
from jax.experimental.pallas import tpu as pltpu

F32 = jnp.float32
BF16 = jnp.bfloat16

RWKV_HEADS = 8
HEAD_DIM = 64
RWKV_WIDTH = RWKV_HEADS * HEAD_DIM
DECAY_LORA = 64
AAA_LORA = 64
GATE_LORA = 128
VRES_LORA = 32
GMLP_BLOCK = 128
GMLP_GROUPS = 4
GMLP_WIDTH = 512
STREAM_CHUNK = 64
N_MOD = 6
RMS_EPS = 1e-6
LN_EPS = 1e-5
GN_EPS = 64e-5
KK_NORM_FLOOR = 1e-12
EXACT_TERMS = 2
MIX_TOKENS = 512
FFN_TOKENS = 512
FFN_CHUNK = 1024

LANES = 128
MXU_WIDTH = 256
HEADS_PER_GROUP = MXU_WIDTH // HEAD_DIM
N_GROUPS = RWKV_HEADS // HEADS_PER_GROUP
SCAN_CHUNK = 64
SIDE_COLS_PER_STEP = 3 * MXU_WIDTH
VMEM_LIMIT_BYTES =56 * 1024 * 1024

COL_LORA = 3 * RWKV_WIDTH
COL_GLORA = COL_LORA + DECAY_LORA + AAA_LORA
RWKV_COLS = COL_GLORA + GATE_LORA
COL_GATES = RWKV_COLS + 2 * GMLP_WIDTH


def _mm(a, b):
    return jnp.dot(a.astype(BF16), b.astype(BF16), preferred_element_type=F32)


def _split_bf16(x, terms):
    parts = []
    for _ in range(terms):
        p = x.astype(BF16)
        parts.append(p)
        x = x - p.astype(F32)
    return parts


def _head_sums(x, seg):
    w = seg.shape[0]
    xb = x.astype(BF16)
    return jnp.concatenate(
        [jnp.dot(xb[:, c0:c0 + w], seg, preferred_element_type=F32)
         for c0 in range(0, x.shape[1], w)], axis=1)


def _mm_exact_rhs(a, m01, terms):
    out = None
    for p in _split_bf16(a, terms):
        t = jnp.dot(p, m01, preferred_element_type=F32)
        out = t if out is None else out + t
    return out


def _mm_exact_lhs(m01, b, terms):
    out = None
    for p in _split_bf16(b, terms):
        t = jnp.dot(m01, p, preferred_element_type=F32)
        out = t if out is None else out + t
    return out


def _gelu_exact(x):
    return 0.5 * x * (1.0 + lax.erf(x * (2.0 ** -0.5)))


def _rms_norm(x, gain):
    ms = jnp.mean(x * x, axis=-1, keepdims=True)
    return (x * lax.rsqrt(ms + RMS_EPS)) * gain


def _mod_kernel(c_ref, w_ref, b_ref, o_ref):
    c = c_ref[...]
    c_act = c * jax.nn.sigmoid(c)
    o_ref[...] = _mm(c_act, w_ref[...]) + b_ref[...]


def _modulation(c, w_ada, b_ada):
    depth, d, _ = w_ada.shape
    batch = c.shape[0]
    return pl.pallas_call(
        _mod_kernel,
        grid=(depth, N_MOD),
        in_specs=[
            pl.BlockSpec((batch, d), lambda l, j: (0, 0)),
            pl.BlockSpec((None, d, d), lambda l, j: (l, 0, j)),
            pl.BlockSpec((None, 1, d), lambda l, j: (l, 0, j)),
        ],
        out_specs=pl.BlockSpec((None, batch, d), lambda l, j: (l, 0, j)),
        out_shape=jax.ShapeDtypeStruct((depth, batch, N_MOD * d), F32),
        compiler_params=pltpu.CompilerParams(
            dimension_semantics=("arbitrary", "arbitrary"),
            vmem_limit_bytes=VMEM_LIMIT_BYTES),
    )(c, w_ada, b_ada.reshape(depth, 1, N_MOD * d))


def _scan_tile(r, lw, k, v, a, b, seg, st_ref):
    ts = r.shape[0]
    n_chunks = ts // SCAN_CHUNK
    w = MXU_WIDTH
    ti = lax.broadcasted_iota(jnp.int32, (ts, ts), 0)
    tj = lax.broadcasted_iota(jnp.int32, (ts, ts), 1)
    tri_blk = ((ti // SCAN_CHUNK == tj // SCAN_CHUNK) & (ti >= tj)).astype(BF16)
    t_row = lax.broadcasted_iota(jnp.int32, (SCAN_CHUNK, MXU_WIDTH), 0)
    g_lane = lax.broadcasted_iota(jnp.int32, (SCAN_CHUNK, MXU_WIDTH), 1)
    t_col = g_lane % HEAD_DIM
    strict = (t_row > t_col).astype(F32)
    incl = (t_row >= t_col).astype(F32)
    eye = (t_row == t_col).astype(F32)
    head_of_lane = g_lane[0:1, :] // HEAD_DIM
    head_masks = [head_of_lane == hh for hh in range(HEADS_PER_GROUP)]

    def bd(x):
        xb = x.astype(BF16)
        return jnp.concatenate([jnp.where(m, xb, jnp.zeros_like(xb)) for m in head_masks], axis=0)

    def dot(x, wgt):
        return jnp.dot(x.astype(BF16), wgt, preferred_element_type=F32)

    def dot_nt(x, wgt):
        return lax.dot_general(x.astype(BF16), wgt, (((1,), (1,)), ((), ())),
                               preferred_element_type=F32)

    def part(z, c, q):
        return z[c * SCAN_CHUNK:(c + 1) * SCAN_CHUNK, q * w:(q + 1) * w]

    cl = _mm_exact_lhs(tri_blk, lw, EXACT_TERMS)
    e_ncl = jnp.exp(-cl)
    a_t = a * jnp.exp(cl - lw)
    r_t = r * jnp.exp(cl)
    b_t = b * e_ncl
    k_t = k * e_ncl
    w_diag = [jnp.concatenate([eye, eye], axis=1)
              * jnp.exp(cl[(c + 1) * SCAN_CHUNK - 1:(c + 1) * SCAN_CHUNK, :]) for c in range(n_chunks)]
    w_diag = jnp.concatenate(w_diag, axis=0)
    w_col = [_mm_exact_rhs(w_diag[:, q * w:(q + 1) * w], seg, EXACT_TERMS)
             for q in range(N_GROUPS)]

    insts = [(c, q) for c in range(n_chunks) for q in range(N_GROUPS)]
    a_i = {i: part(a_t, *i) for i in insts}
    r_i = {i: part(r_t, *i) for i in insts}
    v_i = {i: part(v, *i) for i in insts}
    w_i = {(c, q): w_col[q][c * SCAN_CHUNK:(c + 1) * SCAN_CHUNK, :] for c, q in insts}
    gg = {i: dot_nt(jnp.concatenate([a_i[i], r_i[i], eye], axis=0),
                    jnp.concatenate([bd(part(b_t, *i)), bd(part(k_t, *i))], axis=0))
          for i in insts}
    yield
    l_ab = {i: gg[i][:SCAN_CHUNK, :w] * strict for i in insts}
    l_ak = {i: gg[i][:SCAN_CHUNK, w:] * strict for i in insts}
    m_rb = {i: gg[i][SCAN_CHUNK:2 * SCAN_CHUNK, :w] * incl for i in insts}
    m_rk = {i: gg[i][SCAN_CHUNK:2 * SCAN_CHUNK, w:] * incl for i in insts}
    bh_t = {i: gg[i][2 * SCAN_CHUNK:, :w] * w_i[i] for i in insts}
    kh_t = {i: gg[i][2 * SCAN_CHUNK:, w:] * w_i[i] for i in insts}
    t_inv = {i: eye + l_ab[i] for i in insts}
    pw = {i: dot(l_ab[i], bd(l_ab[i])) for i in insts}
    xv = {i: dot(jnp.concatenate([l_ak[i], m_rk[i], kh_t[i]], axis=0), bd(v_i[i])) for i in insts}
    n = 4
    while n < SCAN_CHUNK:
        yield
        for i in insts:
            both = dot(jnp.concatenate([pw[i], t_inv[i]], axis=0), bd(pw[i]))
            pw[i] = both[:SCAN_CHUNK]
            t_inv[i] = t_inv[i] + both[SCAN_CHUNK:]
        n *= 2
    yield
    for i in insts:
        t_inv[i] = t_inv[i] + dot(t_inv[i], bd(pw[i]))
    yield
    xt = {i: dot(jnp.concatenate([m_rb[i], bh_t[i]], axis=0), bd(t_inv[i])) for i in insts}
    yield
    xpq = {i: dot(xt[i], jnp.concatenate([bd(a_i[i]), bd(xv[i][:SCAN_CHUNK])], axis=1))
           for i in insts}
    yield
    r_phi = {i: jnp.concatenate([r_i[i] + xpq[i][:SCAN_CHUNK, :w], xpq[i][SCAN_CHUNK:, :w]], axis=0)
             for i in insts}
    y_loc = {i: xpq[i][:SCAN_CHUNK, w:] + xv[i][SCAN_CHUNK:2 * SCAN_CHUNK] for i in insts}
    psi_t = {i: xpq[i][SCAN_CHUNK:, w:] + xv[i][2 * SCAN_CHUNK:] for i in insts}
    y_rows = []
    s_t = [st_ref[q] for q in range(N_GROUPS)]
    for c in range(n_chunks):
        y_cols = []
        for q in range(N_GROUPS):
            i = (c, q)
            both = dot(r_phi[i], bd(s_t[q]))
            y_cols.append(both[:SCAN_CHUNK] + y_loc[i])
            s_t[q] = w_i[i] * s_t[q] + both[SCAN_CHUNK:] + psi_t[i]
        y_rows.append(jnp.concatenate(y_cols, axis=1))
        yield
    for q in range(N_GROUPS):
        st_ref[q] = s_t[q]
    return jnp.concatenate(y_rows, axis=0)


def _gmlp_and_gates(h, w_in_ref, lnw_ref, lnb_ref, ws_ref, bst_ref, wbr_gm_ref, ts):
    cw = SIDE_COLS_PER_STEP
    p_cols = []
    for c0 in range(RWKV_COLS, w_in_ref.shape[1], cw):
        p_cols.append(jnp.dot(h, w_in_ref[:, c0:c0 + cw], preferred_element_type=F32))
        yield
    p_side = jnp.concatenate(p_cols, axis=1)
    u_g = _gelu_exact(p_side[:, 0:GMLP_WIDTH])
    yield
    v_g = _gelu_exact(p_side[:, GMLP_WIDTH:2 * GMLP_WIDTH])
    yield
    mu_g = jnp.mean(v_g, axis=-1, keepdims=True)
    vc_g = v_g - mu_g
    var_g = jnp.mean(vc_g * vc_g, axis=-1, keepdims=True)
    v_g = vc_g * lax.rsqrt(var_g + LN_EPS) * lnw_ref[...] + lnb_ref[...]
    yield
    bi = lax.broadcasted_iota(jnp.int32, (GMLP_BLOCK, GMLP_BLOCK), 0)
    bj = lax.broadcasted_iota(jnp.int32, (GMLP_BLOCK, GMLP_BLOCK), 1)
    causal = (bi // STREAM_CHUNK) >= (bj // STREAM_CHUNK)
    gd = GMLP_WIDTH // GMLP_GROUPS
    bst = bst_ref[...]
    sv_cols = []
    for gi in range(GMLP_GROUPS):
        w_g = jnp.where(causal, ws_ref[gi], 0.0)
        bias = bst[:, gi:gi + 1]
        v_blocks = jnp.concatenate(
            [v_g[nb * GMLP_BLOCK:(nb + 1) * GMLP_BLOCK, gi * gd:(gi + 1) * gd]
             for nb in range(ts // GMLP_BLOCK)], axis=1)
        sv = _mm(w_g, v_blocks) + bias
        sv_cols.append(jnp.concatenate(
            [sv[:, nb * gd:(nb + 1) * gd] for nb in range(ts // GMLP_BLOCK)], axis=0))
    y_gm = u_g * jnp.concatenate(sv_cols, axis=1)
    yield
    gate_cols = []
    for c0 in range(2 * GMLP_WIDTH, p_side.shape[1], GMLP_WIDTH):
        gate_cols.append(jax.nn.sigmoid(p_side[:, c0:c0 + GMLP_WIDTH]))
        yield
    gates = jnp.concatenate(gate_cols, axis=1)
    d = gates.shape[1] // 2
    yield
    br_cols = []
    for c0 in range(0, d, MXU_WIDTH):
        br_cols.append(_mm(y_gm, wbr_gm_ref[:, c0:c0 + MXU_WIDTH]))
        yield
    return gates[:, 0:d], gates[:, d:] * jnp.concatenate(br_cols, axis=1)


def _interleave(main, side):
    out = [None, None]
    live = [main, side]
    while any(g is not None for g in live):
        for n, g in enumerate(live):
            if g is not None:
                try:
                    next(g)
                except StopIteration as stop:
                    out[n], live[n] = stop.value, None
    return out


def _mix_kernel(has_vres, ts, *refs):
    it = iter(refs)
    x_ref, mod_ref = next(it), next(it)
    vfirst_ref = next(it) if has_vres else None
    (g1_ref, w_in_ref, mu_ref, w0_ref, w2wa_ref, a0_ref, w2g_ref, kk_ref, ka_ref, rk_ref,
     gnw_ref, gnb_ref) = (next(it) for _ in range(12))
    if has_vres:
        v0_ref, w1r_ref, w2r_ref = next(it), next(it), next(it)
    (lnw_ref, lnb_ref, ws_ref, bst_ref, wbr_rw_ref, wbr_gm_ref, w_out_ref, seg_ref) = (
        next(it) for _ in range(8))
    xo_ref = next(it)
    vo_ref = None if has_vres else next(it)
    st_ref, carry_ref = next(it), next(it)

    d = x_ref.shape[-1]
    j = pl.program_id(1)

    @pl.when(j == 0)
    def _():
        st_ref[...] = jnp.zeros_like(st_ref)
        carry_ref[...] = jnp.zeros_like(carry_ref)

    x = x_ref[...]
    mod = mod_ref[...]
    sh1, sc1, gt1 = mod[:, 0:d], mod[:, d:2 * d], mod[:, 2 * d:3 * d]
    h = (_rms_norm(x, g1_ref[...]) * (1.0 + sc1) + sh1).astype(BF16)

    seg = seg_ref[...]

    def rwkv_branch():
        p_rw = jnp.dot(h, w_in_ref[:, 0:RWKV_COLS], preferred_element_type=F32)
        yield
        row = lax.broadcasted_iota(jnp.int32, (ts, 1), 0)
        prev = jnp.where(row == 0, carry_ref[...], pltpu.roll(p_rw, 1, 0))
        carry_ref[...] = p_rw[ts - 1:ts, :]
        xs = p_rw + (prev - p_rw) * mu_ref[...]
        r = xs[:, 0:RWKV_WIDTH]
        k = xs[:, RWKV_WIDTH:2 * RWKV_WIDTH]
        v = xs[:, 2 * RWKV_WIDTH:3 * RWKV_WIDTH]
        lo = xs[:, COL_LORA:COL_GLORA]
        g_lo = xs[:, COL_GLORA:RWKV_COLS]
        yield
        lane = lax.broadcasted_iota(jnp.int32, (1, LANES), 1)
        lo_act = jnp.where(lane < DECAY_LORA, jnp.tanh(lo), lo)
        wa = _mm(lo_act, w2wa_ref[...])
        w_log = -jax.nn.softplus(-(w0_ref[...] + wa[:, 0:RWKV_WIDTH])) - 0.5
        a = jax.nn.sigmoid(a0_ref[...] + wa[:, RWKV_WIDTH:])
        gate = _mm(jax.nn.sigmoid(g_lo), w2g_ref[...])
        yield
        if has_vres:
            vmix = jax.nn.sigmoid(v0_ref[...] + _mm(_mm(v, w1r_ref[...]), w2r_ref[...]))
            v = v + (vfirst_ref[...] - v) * vmix
        else:
            vo_ref[...] = v
        kk = k * kk_ref[...]
        kk = kk * lax.rsqrt(jnp.maximum(_head_sums(kk * kk, seg), KK_NORM_FLOOR ** 2))
        k = k * (1.0 + (a - 1.0) * ka_ref[...])
        bonus = _head_sums(r * k * rk_ref[...], seg) * v
        yield
        y = yield from _scan_tile(r, -jnp.exp(w_log), k, v, -kk, kk * a, seg, st_ref)
        return y, bonus, gate

    (y, bonus, gate), (gate_rw, gm_term) = _interleave(
        rwkv_branch(),
        _gmlp_and_gates(h, w_in_ref, lnw_ref, lnb_ref, ws_ref, bst_ref, wbr_gm_ref, ts))
    mean = _head_sums(y, seg) * (1.0 / HEAD_DIM)
    yc = y - mean
    var = _head_sums(yc * yc, seg) * (1.0 / HEAD_DIM)
    y = yc * lax.rsqrt(var + GN_EPS) * gnw_ref[...] + gnb_ref[...]
    y_rw = (y + bonus) * gate

    merged = gate_rw * _mm(y_rw, wbr_rw_ref[...]) + gm_term
    xo_ref[...] = x + gt1 * _mm(merged, w_out_ref[...])


def _const_spec(shape, layer=None):
    if layer is None:
        return pl.BlockSpec(shape, lambda b, j: (0,) * len(shape), pipeline_mode=pl.Buffered(1))
    return pl.BlockSpec((None,) + shape, lambda b, j: (layer,) + (0,) * len(shape),
                        pipeline_mode=pl.Buffered(1))


def _token_mix(layer, x, mod, v_first, params, ts):
    batch, seq, d = x.shape
    has_vres = layer > 0
    tile = lambda w: pl.BlockSpec((None, ts, w), lambda b, j: (b, j, 0))
    row512 = _const_spec((1, RWKV_WIDTH), layer)
    p = params
    operands = [x, mod]
    in_specs = [tile(d), pl.BlockSpec((None, 1, N_MOD * d), lambda b, j: (b, 0, 0))]
    if has_vres:
        operands.append(v_first)
        in_specs.append(tile(RWKV_WIDTH))
    operands += [p["norm1_g"], p["w_in"], p["mu_shift"], p["w0_decay"], p["w2_wa"], p["a0"],
                 p["w2_gate"], p["k_k"], p["k_a"], p["r_k"], p["gn_w"], p["gn_b"]]
    in_specs += [_const_spec((1, d), layer), _const_spec(p["w_in"].shape[1:], layer),
                 _const_spec((1, RWKV_COLS), layer), row512,
                 _const_spec(p["w2_wa"].shape[1:], layer), row512,
                 _const_spec(p["w2_gate"].shape[1:], layer), row512, row512, row512, row512, row512]
    if has_vres:
        operands += [p["v0_res"], p["w1_res"], p["w2_res"]]
        in_specs += [_const_spec((1, RWKV_WIDTH), layer - 1),
                     _const_spec(p["w1_res"].shape[1:], layer - 1),
                     _const_spec(p["w2_res"].shape[1:], layer - 1)]
    operands += [p["ln_gmlp_w"], p["ln_gmlp_b"], p["w_spatial"], p["b_spatial_t"],
                 p["w_br_rwkv"], p["w_br_gmlp"], p["w_out"], p["seg_ones"]]
    in_specs += [_const_spec((1, GMLP_WIDTH), layer), _const_spec((1, GMLP_WIDTH), layer),
                 _const_spec(p["w_spatial"].shape[1:], layer),
                 _const_spec(p["b_spatial_t"].shape[1:], layer),
                 _const_spec(p["w_br_rwkv"].shape[1:], layer),
                 _const_spec(p["w_br_gmlp"].shape[1:], layer),
                 _const_spec(p["w_out"].shape[1:], layer),
                 _const_spec(p["seg_ones"].shape)]
    out_shape = [jax.ShapeDtypeStruct((batch, seq, d), F32)]
    out_specs = [tile(d)]
    if not has_vres:
        out_shape.append(jax.ShapeDtypeStruct((batch, seq, RWKV_WIDTH), F32))
        out_specs.append(tile(RWKV_WIDTH))
    scratch = [pltpu.VMEM((N_GROUPS, HEAD_DIM, MXU_WIDTH), F32), pltpu.VMEM((1, RWKV_COLS), F32)]
    outs = pl.pallas_call(
        functools.partial(_mix_kernel, has_vres, ts),
        grid=(batch, seq // ts),
        in_specs=in_specs,
        out_specs=out_specs,
        out_shape=out_shape,
        scratch_shapes=scratch,
        compiler_params=pltpu.CompilerParams(
            dimension_semantics=("arbitrary", "arbitrary"),
            vmem_limit_bytes=VMEM_LIMIT_BYTES),
    )(*operands)
    return outs if has_vres else (outs[0], outs[1])


def _ffn_kernel(final, ff_chunk, *refs):
    if final:
        x_ref, mod_ref, g2_ref, w1_ref, w2_ref, fg_ref, o_ref = refs
    else:
        x_ref, mod_ref, g2_ref, w1_ref, w2_ref, o_ref = refs
    d = x_ref.shape[-1]
    x = x_ref[...]
    mod = mod_ref[...]
    sh2, sc2, gt2 = mod[:, 3 * d:4 * d], mod[:, 4 * d:5 * d], mod[:, 5 * d:6 * d]
    h = (_rms_norm(x, g2_ref[...]) * (1.0 + sc2) + sh2).astype(BF16)
    acc = jnp.zeros_like(x)
    for c0 in range(0, w1_ref.shape[1], ff_chunk):
        t = jnp.maximum(jnp.dot(h, w1_ref[:, c0:c0 + ff_chunk], preferred_element_type=F32), 0.0)
        acc = acc + _mm(t * t, w2_ref[c0:c0 + ff_chunk, :])
    xn = x + gt2 * acc
    if final:
        xn = _rms_norm(xn, fg_ref[...])
    o_ref[...] = xn


def _channel_mix(layer, x, mod, params, final, tm, ff_chunk=FFN_CHUNK):
    batch, seq, d = x.shape
    tile = pl.BlockSpec((None, tm, d), lambda b, j: (b, j, 0))
    operands = [x, mod, params["norm2_g"], params["w_ff1"], params["w_ff2"]]
    in_specs = [tile, pl.BlockSpec((None, 1, N_MOD * d), lambda b, j: (b, 0, 0)),
                _const_spec((1, d), layer), _const_spec(params["w_ff1"].shape[1:], layer),
                _const_spec(params["w_ff2"].shape[1:], layer)]
    if final:
        operands.append(params["final_g"])
        in_specs.append(_const_spec((1, d)))
    return pl.pallas_call(
        functools.partial(_ffn_kernel, final, ff_chunk),
        grid=(batch, seq // tm),
        in_specs=in_specs,
        out_specs=tile,
        out_shape=jax.ShapeDtypeStruct((batch, seq, d), F32),
        compiler_params=pltpu.CompilerParams(
            dimension_semantics=("arbitrary", "arbitrary"),
            vmem_limit_bytes=VMEM_LIMIT_BYTES),
    )(*operands)


def kernel(x, c, w_ada, b_ada, norm1_g, norm2_g, w_in, mu_shift, w0_decay, w2_decay, a0, w2_aaa, w2_gate, k_k, k_a, r_k, gn_w, gn_b, v0_res, w1_res, w2_res, ln_gmlp_w, ln_gmlp_b, w_spatial, b_spatial, w_br_rwkv, w_br_gmlp, w_out, w_ff1, w_ff2, final_g):
    depth = w_in.shape[0]
    batch, seq, d = x.shape
    ts = min(MIX_TOKENS, seq)
    tm = min(FFN_TOKENS, seq)
    row = lambda t: t.reshape(t.shape[0], 1, -1)
    zeros = lambda *s: jnp.zeros(s, F32)
    w2_wa = jnp.concatenate([
        jnp.concatenate([w2_decay, zeros(depth, DECAY_LORA, RWKV_WIDTH)], axis=2),
        jnp.concatenate([zeros(depth, AAA_LORA, RWKV_WIDTH), w2_aaa], axis=2)], axis=1)
    pad = LANES - VRES_LORA
    head_id = jnp.arange(MXU_WIDTH) // HEAD_DIM
    params = dict(
        norm1_g=row(norm1_g), norm2_g=row(norm2_g), w_in=w_in.astype(BF16), mu_shift=row(mu_shift),
        w0_decay=row(w0_decay), w2_wa=w2_wa.astype(BF16), a0=row(a0), w2_gate=w2_gate.astype(BF16),
        k_k=row(k_k), k_a=row(k_a), r_k=row(r_k), gn_w=row(gn_w), gn_b=row(gn_b),
        v0_res=row(v0_res),
        w1_res=jnp.pad(w1_res, ((0, 0), (0, 0), (0, pad))).astype(BF16),
        w2_res=jnp.pad(w2_res, ((0, 0), (0, pad), (0, 0))).astype(BF16),
        ln_gmlp_w=row(ln_gmlp_w), ln_gmlp_b=row(ln_gmlp_b), w_spatial=w_spatial,
        b_spatial_t=jnp.swapaxes(b_spatial, 1, 2),
        w_br_rwkv=w_br_rwkv.astype(BF16), w_br_gmlp=w_br_gmlp.astype(BF16),
        w_out=w_out.astype(BF16), w_ff1=w_ff1.astype(BF16), w_ff2=w_ff2.astype(BF16),
        final_g=final_g.reshape(1, d),
        seg_ones=(head_id[:, None] == head_id[None, :]).astype(BF16),
    )
    mod = _modulation(c, w_ada, b_ada)
    v_first = None
    for layer in range(depth):
        mod_l = mod[layer].reshape(batch, 1, N_MOD * d)
        if layer == 0:
            x, v_first = _token_mix(layer, x, mod_l, None, params, ts)
        else:
            (x,) = _token_mix(layer, x, mod_l, v_first, params, ts)
        x = _channel_mix(layer, x, mod_l, params, layer == depth - 1, tm)
    return x
```

```python
import functools

import jax
import jax.numpy as jnp
from jax import lax
from jax.experimental import pallas as pl
from jax.experimental.pallas import tpu as pltpu

F32 = jnp.float32
BF16 = jnp.bfloat16

RWKV_HEADS = 8
HEAD_DIM = 64
RWKV_WIDTH = RWKV_HEADS * HEAD_DIM
DECAY_LORA = 64
AAA_LORA = 64
GATE_LORA = 128
VRES_LORA = 32
GMLP_BLOCK = 128
GMLP_GROUPS = 4
GMLP_WIDTH = 512
STREAM_CHUNK = 64
N_MOD = 6
RMS_EPS = 1e-6
LN_EPS = 1e-5
GN_EPS = 64e-5
KK_NORM_FLOOR = 1e-12
EXACT_TERMS = 2
MIX_TOKENS = 512
FFN_TOKENS = 512
FFN_CHUNK = 1024

LANES = 128
MXU_WIDTH = 256
HEADS_PER_GROUP = MXU_WIDTH // HEAD_DIM
N_GROUPS = RWKV_HEADS // HEADS_PER_GROUP
SCAN_CHUNK = 64
INV_BASE = 8
SIDE_COLS_PER_STEP = 3 * MXU_WIDTH
VMEM_LIMIT_BYTES =56 * 1024 * 1024

COL_LORA = 3 * RWKV_WIDTH
COL_GLORA = COL_LORA + DECAY_LORA + AAA_LORA
RWKV_COLS = COL_GLORA + GATE_LORA
COL_GATES = RWKV_COLS + 2 * GMLP_WIDTH


def _mm(a, b):
    return jnp.dot(a.astype(BF16), b.astype(BF16), preferred_element_type=F32)


def _split_bf16(x, terms):
    parts = []
    for _ in range(terms):
        p = x.astype(BF16)
        parts.append(p)
        x = x - p.astype(F32)
    return parts


def _head_sums(x, seg):
    w = seg.shape[0]
    xb = x.astype(BF16)
    return jnp.concatenate(
        [jnp.dot(xb[:, c0:c0 + w], seg, preferred_element_type=F32)
         for c0 in range(0, x.shape[1], w)], axis=1)


def _mm_exact_rhs(a, m01, terms):
    out = None
    for p in _split_bf16(a, terms):
        t = jnp.dot(p, m01, preferred_element_type=F32)
        out = t if out is None else out + t
    return out


def _mm_exact_lhs(m01, b, terms):
    out = None
    for p in _split_bf16(b, terms):
        t = jnp.dot(m01, p, preferred_element_type=F32)
        out = t if out is None else out + t
    return out


def _gelu_exact(x):
    return 0.5 * x * (1.0 + lax.erf(x * (2.0 ** -0.5)))


def _rms_norm(x, gain):
    ms = jnp.mean(x * x, axis=-1, keepdims=True)
    return (x * lax.rsqrt(ms + RMS_EPS)) * gain


def _mod_kernel(c_ref, w_ref, b_ref, o_ref):
    c = c_ref[...]
    c_act = c * jax.nn.sigmoid(c)
    o_ref[...] = _mm(c_act, w_ref[...]) + b_ref[...]


def _modulation(c, w_ada, b_ada):
    depth, d, _ = w_ada.shape
    batch = c.shape[0]
    return pl.pallas_call(
        _mod_kernel,
        grid=(depth, N_MOD),
        in_specs=[
            pl.BlockSpec((batch, d), lambda l, j: (0, 0)),
            pl.BlockSpec((None, d, d), lambda l, j: (l, 0, j)),
            pl.BlockSpec((None, 1, d), lambda l, j: (l, 0, j)),
        ],
        out_specs=pl.BlockSpec((None, batch, d), lambda l, j: (l, 0, j)),
        out_shape=jax.ShapeDtypeStruct((depth, batch, N_MOD * d), F32),
        compiler_params=pltpu.CompilerParams(
            dimension_semantics=("arbitrary", "arbitrary"),
            vmem_limit_bytes=VMEM_LIMIT_BYTES),
    )(c, w_ada, b_ada.reshape(depth, 1, N_MOD * d))


def _scan_tile(r, lw, k, v, a, b, seg, st_ref):
    ts = r.shape[0]
    n_chunks = ts // SCAN_CHUNK
    w = MXU_WIDTH
    ti = lax.broadcasted_iota(jnp.int32, (ts, ts), 0)
    tj = lax.broadcasted_iota(jnp.int32, (ts, ts), 1)
    tri_blk = ((ti // SCAN_CHUNK == tj // SCAN_CHUNK) & (ti >= tj)).astype(BF16)
    t_row = lax.broadcasted_iota(jnp.int32, (SCAN_CHUNK, MXU_WIDTH), 0)
    g_lane = lax.broadcasted_iota(jnp.int32, (SCAN_CHUNK, MXU_WIDTH), 1)
    t_col = g_lane % HEAD_DIM
    strict = (t_row > t_col).astype(F32)
    incl = (t_row >= t_col).astype(F32)
    eye = (t_row == t_col).astype(F32)
    head_of_lane = g_lane[0:1, :] // HEAD_DIM
    head_masks = [head_of_lane == hh for hh in range(HEADS_PER_GROUP)]

    def bd(x):
        xb = x.astype(BF16)
        return jnp.concatenate([jnp.where(m, xb, jnp.zeros_like(xb)) for m in head_masks], axis=0)

    def dot(x, wgt):
        return jnp.dot(x.astype(BF16), wgt, preferred_element_type=F32)

    def dot_nt(x, wgt):
        return lax.dot_general(x.astype(BF16), wgt, (((1,), (1,)), ((), ())),
                               preferred_element_type=F32)

    def part(z, c, q):
        return z[c * SCAN_CHUNK:(c + 1) * SCAN_CHUNK, q * w:(q + 1) * w]

    cl = _mm_exact_lhs(tri_blk, lw, EXACT_TERMS)
    e_ncl = jnp.exp(-cl)
    a_t = (a * jnp.exp(cl - lw)).astype(BF16)
    r_t = r * jnp.exp(cl)
    b_t = (b * e_ncl).astype(BF16)
    k_t = (k * e_ncl).astype(BF16)
    v = v.astype(BF16)
    w_diag = [jnp.concatenate([eye, eye], axis=1)
              * jnp.exp(cl[(c + 1) * SCAN_CHUNK - 1:(c + 1) * SCAN_CHUNK, :]) for c in range(n_chunks)]
    w_diag = jnp.concatenate(w_diag, axis=0)
    w_col = [_mm_exact_rhs(w_diag[:, q * w:(q + 1) * w], seg, EXACT_TERMS)
             for q in range(N_GROUPS)]

    insts = [(c, q) for c in range(n_chunks) for q in range(N_GROUPS)]
    a_i = {i: part(a_t, *i) for i in insts}
    r_i = {i: part(r_t, *i) for i in insts}
    v_i = {i: part(v, *i) for i in insts}
    w_i = {(c, q): w_col[q][c * SCAN_CHUNK:(c + 1) * SCAN_CHUNK, :] for c, q in insts}
    eye_b = eye.astype(BF16)
    gg = {i: dot_nt(jnp.concatenate([a_i[i], r_i[i].astype(BF16), eye_b], axis=0),
                    jnp.concatenate([bd(part(b_t, *i)), bd(part(k_t, *i))], axis=0))
          for i in insts}
    yield
    l_ab = {i: gg[i][:SCAN_CHUNK, :w] * strict for i in insts}
    rows_v = {i: jnp.concatenate([gg[i][:SCAN_CHUNK, w:] * strict,
                                  gg[i][SCAN_CHUNK:2 * SCAN_CHUNK, w:] * incl,
                                  gg[i][2 * SCAN_CHUNK:, w:] * w_i[i]], axis=0).astype(BF16)
              for i in insts}
    rows_t = {i: jnp.concatenate([gg[i][SCAN_CHUNK:2 * SCAN_CHUNK, :w] * incl,
                                  gg[i][2 * SCAN_CHUNK:, :w] * w_i[i]], axis=0).astype(BF16)
              for i in insts}
    same_block = lambda size: (t_row // size) == (t_col // size)
    l_d = {i: jnp.where(same_block(INV_BASE), l_ab[i], 0.0).astype(BF16) for i in insts}
    t_inv = {i: eye + l_d[i] for i in insts}
    pw = {i: dot(l_d[i], bd(l_d[i])).astype(BF16) for i in insts}
    xv = {i: dot(rows_v[i], bd(v_i[i])) for i in insts}
    n = 4
    while n < INV_BASE:
        yield
        for i in insts:
            both = dot(jnp.concatenate([pw[i], t_inv[i].astype(BF16)], axis=0), bd(pw[i]))
            pw[i] = both[:SCAN_CHUNK].astype(BF16)
            t_inv[i] = t_inv[i] + both[SCAN_CHUNK:]
        n *= 2
    yield
    for i in insts:
        t_inv[i] = t_inv[i] + dot(t_inv[i], bd(pw[i]))
    size = INV_BASE
    while size < SCAN_CHUNK:
        lower_left = same_block(2 * size) & ~same_block(size)
        yield
        tl = {i: dot(t_inv[i], bd(jnp.where(lower_left, l_ab[i], 0.0))) for i in insts}
        yield
        for i in insts:
            t_inv[i] = t_inv[i] + dot(tl[i], bd(t_inv[i]))
        size *= 2
    yield
    xt = {i: dot(rows_t[i], bd(t_inv[i])) for i in insts}
    yield
    xpq = {i: dot(xt[i], jnp.concatenate([bd(a_i[i]), bd(xv[i][:SCAN_CHUNK])], axis=1))
           for i in insts}
    yield
    r_phi = {i: jnp.concatenate([r_i[i] + xpq[i][:SCAN_CHUNK, :w], xpq[i][SCAN_CHUNK:, :w]], axis=0)
             for i in insts}
    y_loc = {i: xpq[i][:SCAN_CHUNK, w:] + xv[i][SCAN_CHUNK:2 * SCAN_CHUNK] for i in insts}
    psi_t = {i: xpq[i][SCAN_CHUNK:, w:] + xv[i][2 * SCAN_CHUNK:] for i in insts}
    y_rows = []
    s_t = [st_ref[q] for q in range(N_GROUPS)]
    for c in range(n_chunks):
        y_cols = []
        for q in range(N_GROUPS):
            i = (c, q)
            both = dot(r_phi[i], bd(s_t[q]))
            y_cols.append(both[:SCAN_CHUNK] + y_loc[i])
            s_t[q] = w_i[i] * s_t[q] + both[SCAN_CHUNK:] + psi_t[i]
        y_rows.append(jnp.concatenate(y_cols, axis=1))
        yield
    for q in range(N_GROUPS):
        st_ref[q] = s_t[q]
    return jnp.concatenate(y_rows, axis=0)


def _gmlp_and_gates(h, w_in_ref, lnw_ref, lnb_ref, ws_ref, bst_ref, wbr_gm_ref, ts):
    cw = SIDE_COLS_PER_STEP
    p_cols = []
    for c0 in range(RWKV_COLS, w_in_ref.shape[1], cw):
        p_cols.append(jnp.dot(h, w_in_ref[:, c0:c0 + cw], preferred_element_type=F32))
        yield
    p_side = jnp.concatenate(p_cols, axis=1)
    z_cols = []
    for c0 in range(0, 2 * GMLP_WIDTH, MXU_WIDTH):
        z_cols.append(_gelu_exact(p_side[:, c0:c0 + MXU_WIDTH]))
        yield
    half = len(z_cols) // 2
    u_g = jnp.concatenate(z_cols[:half], axis=1)
    v_g = jnp.concatenate(z_cols[half:], axis=1)
    mu_g = jnp.mean(v_g, axis=-1, keepdims=True)
    vc_g = v_g - mu_g
    var_g = jnp.mean(vc_g * vc_g, axis=-1, keepdims=True)
    v_g = vc_g * lax.rsqrt(var_g + LN_EPS) * lnw_ref[...] + lnb_ref[...]
    yield
    bi = lax.broadcasted_iota(jnp.int32, (GMLP_BLOCK, GMLP_BLOCK), 0)
    bj = lax.broadcasted_iota(jnp.int32, (GMLP_BLOCK, GMLP_BLOCK), 1)
    causal = (bi // STREAM_CHUNK) >= (bj // STREAM_CHUNK)
    gd = GMLP_WIDTH // GMLP_GROUPS
    bst = bst_ref[...]
    sv_cols = []
    for gi in range(GMLP_GROUPS):
        w_g = jnp.where(causal, ws_ref[gi], 0.0)
        bias = bst[:, gi:gi + 1]
        v_blocks = jnp.concatenate(
            [v_g[nb * GMLP_BLOCK:(nb + 1) * GMLP_BLOCK, gi * gd:(gi + 1) * gd]
             for nb in range(ts // GMLP_BLOCK)], axis=1)
        sv = _mm(w_g, v_blocks) + bias
        sv_cols.append(jnp.concatenate(
            [sv[:, nb * gd:(nb + 1) * gd] for nb in range(ts // GMLP_BLOCK)], axis=0))
    yield
    y_gm = u_g * jnp.concatenate(sv_cols, axis=1)
    yield
    gate_cols = []
    for c0 in range(2 * GMLP_WIDTH, p_side.shape[1], GMLP_WIDTH):
        gate_cols.append(jax.nn.sigmoid(p_side[:, c0:c0 + GMLP_WIDTH]))
        yield
    gates = jnp.concatenate(gate_cols, axis=1)
    d = gates.shape[1] // 2
    yield
    br_cols = []
    for c0 in range(0, d, MXU_WIDTH):
        br_cols.append(_mm(y_gm, wbr_gm_ref[:, c0:c0 + MXU_WIDTH]))
        yield
    return gates[:, 0:d], gates[:, d:] * jnp.concatenate(br_cols, axis=1)


def _interleave(main, side):
    out = [None, None]
    live = [main, side]
    while any(g is not None for g in live):
        for n, g in enumerate(live):
            if g is not None:
                try:
                    next(g)
                except StopIteration as stop:
                    out[n], live[n] = stop.value, None
    return out


def _mix_kernel(has_vres, ts, *refs):
    it = iter(refs)
    x_ref, mod_ref = next(it), next(it)
    vfirst_ref = next(it) if has_vres else None
    (g1_ref, w_in_ref, mu_ref, w0_ref, w2wa_ref, a0_ref, w2g_ref, kk_ref, ka_ref, rk_ref,
     gnw_ref, gnb_ref) = (next(it) for _ in range(12))
    if has_vres:
        v0_ref, w1r_ref, w2r_ref = next(it), next(it), next(it)
    (lnw_ref, lnb_ref, ws_ref, bst_ref, wbr_rw_ref, wbr_gm_ref, w_out_ref, seg_ref) = (
        next(it) for _ in range(8))
    xo_ref = next(it)
    vo_ref = None if has_vres else next(it)
    st_ref, carry_ref = next(it), next(it)

    d = x_ref.shape[-1]
    j = pl.program_id(1)

    @pl.when(j == 0)
    def _():
        st_ref[...] = jnp.zeros_like(st_ref)
        carry_ref[...] = jnp.zeros_like(carry_ref)

    x = x_ref[...]
    mod = mod_ref[...]
    sh1, sc1, gt1 = mod[:, 0:d], mod[:, d:2 * d], mod[:, 2 * d:3 * d]
    h = (_rms_norm(x, g1_ref[...]) * (1.0 + sc1) + sh1).astype(BF16)

    seg = seg_ref[...]

    def rwkv_branch():
        p_rw = jnp.dot(h, w_in_ref[:, 0:RWKV_COLS], preferred_element_type=F32)
        yield
        row = lax.broadcasted_iota(jnp.int32, (ts, 1), 0)
        prev = jnp.where(row == 0, carry_ref[...], pltpu.roll(p_rw, 1, 0))
        carry_ref[...] = p_rw[ts - 1:ts, :]
        xs = p_rw + (prev - p_rw) * mu_ref[...]
        r = xs[:, 0:RWKV_WIDTH]
        k = xs[:, RWKV_WIDTH:2 * RWKV_WIDTH]
        v = xs[:, 2 * RWKV_WIDTH:3 * RWKV_WIDTH]
        lo = xs[:, COL_LORA:COL_GLORA]
        g_lo = xs[:, COL_GLORA:RWKV_COLS]
        yield
        lane = lax.broadcasted_iota(jnp.int32, (1, LANES), 1)
        lo_act = jnp.where(lane < DECAY_LORA, jnp.tanh(lo), lo)
        wa = _mm(lo_act, w2wa_ref[...])
        w_log = -jax.nn.softplus(-(w0_ref[...] + wa[:, 0:RWKV_WIDTH])) - 0.5
        a = jax.nn.sigmoid(a0_ref[...] + wa[:, RWKV_WIDTH:])
        gate = _mm(jax.nn.sigmoid(g_lo), w2g_ref[...])
        yield
        if has_vres:
            vmix = jax.nn.sigmoid(v0_ref[...] + _mm(_mm(v, w1r_ref[...]), w2r_ref[...]))
            v = v + (vfirst_ref[...] - v) * vmix
        else:
            vo_ref[...] = v
        kk = k * kk_ref[...]
        kk = kk * lax.rsqrt(jnp.maximum(_head_sums(kk * kk, seg), KK_NORM_FLOOR ** 2))
        k = k * (1.0 + (a - 1.0) * ka_ref[...])
        bonus = _head_sums(r * k * rk_ref[...], seg) * v
        yield
        y = yield from _scan_tile(r, -jnp.exp(w_log), k, v, -kk, kk * a, seg, st_ref)
        return y, bonus, gate

    (y, bonus, gate), (gate_rw, gm_term) = _interleave(
        rwkv_branch(),
        _gmlp_and_gates(h, w_in_ref, lnw_ref, lnb_ref, ws_ref, bst_ref, wbr_gm_ref, ts))
    mean = _head_sums(y, seg) * (1.0 / HEAD_DIM)
    yc = y - mean
    var = _head_sums(yc * yc, seg) * (1.0 / HEAD_DIM)
    y = yc * lax.rsqrt(var + GN_EPS) * gnw_ref[...] + gnb_ref[...]
    y_rw = (y + bonus) * gate

    merged = gate_rw * _mm(y_rw, wbr_rw_ref[...]) + gm_term
    xo_ref[...] = x + gt1 * _mm(merged, w_out_ref[...])


def _const_spec(shape, layer=None):
    if layer is None:
        return pl.BlockSpec(shape, lambda b, j: (0,) * len(shape), pipeline_mode=pl.Buffered(1))
    return pl.BlockSpec((None,) + shape, lambda b, j: (layer,) + (0,) * len(shape),
                        pipeline_mode=pl.Buffered(1))


def _token_mix(layer, x, mod, v_first, params, ts):
    batch, seq, d = x.shape
    has_vres = layer > 0
    tile = lambda w: pl.BlockSpec((None, ts, w), lambda b, j: (b, j, 0))
    row512 = _const_spec((1, RWKV_WIDTH), layer)
    p = params
    operands = [x, mod]
    in_specs = [tile(d), pl.BlockSpec((None, 1, N_MOD * d), lambda b, j: (b, 0, 0))]
    if has_vres:
        operands.append(v_first)
        in_specs.append(tile(RWKV_WIDTH))
    operands += [p["norm1_g"], p["w_in"], p["mu_shift"], p["w0_decay"], p["w2_wa"], p["a0"],
                 p["w2_gate"], p["k_k"], p["k_a"], p["r_k"], p["gn_w"], p["gn_b"]]
    in_specs += [_const_spec((1, d), layer), _const_spec(p["w_in"].shape[1:], layer),
                 _const_spec((1, RWKV_COLS), layer), row512,
                 _const_spec(p["w2_wa"].shape[1:], layer), row512,
                 _const_spec(p["w2_gate"].shape[1:], layer), row512, row512, row512, row512, row512]
    if has_vres:
        operands += [p["v0_res"], p["w1_res"], p["w2_res"]]
        in_specs += [_const_spec((1, RWKV_WIDTH), layer - 1),
                     _const_spec(p["w1_res"].shape[1:], layer - 1),
                     _const_spec(p["w2_res"].shape[1:], layer - 1)]
    operands += [p["ln_gmlp_w"], p["ln_gmlp_b"], p["w_spatial"], p["b_spatial_t"],
                 p["w_br_rwkv"], p["w_br_gmlp"], p["w_out"], p["seg_ones"]]
    in_specs += [_const_spec((1, GMLP_WIDTH), layer), _const_spec((1, GMLP_WIDTH), layer),
                 _const_spec(p["w_spatial"].shape[1:], layer),
                 _const_spec(p["b_spatial_t"].shape[1:], layer),
                 _const_spec(p["w_br_rwkv"].shape[1:], layer),
                 _const_spec(p["w_br_gmlp"].shape[1:], layer),
                 _const_spec(p["w_out"].shape[1:], layer),
                 _const_spec(p["seg_ones"].shape)]
    out_shape = [jax.ShapeDtypeStruct((batch, seq, d), F32)]
    out_specs = [tile(d)]
    if not has_vres:
        out_shape.append(jax.ShapeDtypeStruct((batch, seq, RWKV_WIDTH), F32))
        out_specs.append(tile(RWKV_WIDTH))
    scratch = [pltpu.VMEM((N_GROUPS, HEAD_DIM, MXU_WIDTH), F32), pltpu.VMEM((1, RWKV_COLS), F32)]
    outs = pl.pallas_call(
        functools.partial(_mix_kernel, has_vres, ts),
        grid=(batch, seq // ts),
        in_specs=in_specs,
        out_specs=out_specs,
        out_shape=out_shape,
        scratch_shapes=scratch,
        compiler_params=pltpu.CompilerParams(
            dimension_semantics=("arbitrary", "arbitrary"),
            vmem_limit_bytes=VMEM_LIMIT_BYTES),
    )(*operands)
    return outs if has_vres else (outs[0], outs[1])


def _ffn_kernel(final, ff_chunk, *refs):
    if final:
        x_ref, mod_ref, g2_ref, w1_ref, w2_ref, fg_ref, o_ref = refs
    else:
        x_ref, mod_ref, g2_ref, w1_ref, w2_ref, o_ref = refs
    d = x_ref.shape[-1]
    x = x_ref[...]
    mod = mod_ref[...]
    sh2, sc2, gt2 = mod[:, 3 * d:4 * d], mod[:, 4 * d:5 * d], mod[:, 5 * d:6 * d]
    h = (_rms_norm(x, g2_ref[...]) * (1.0 + sc2) + sh2).astype(BF16)
    acc = jnp.zeros_like(x)
    for c0 in range(0, w1_ref.shape[1], ff_chunk):
        t = jnp.maximum(jnp.dot(h, w1_ref[:, c0:c0 + ff_chunk], preferred_element_type=F32), 0.0)
        acc = acc + _mm(t * t, w2_ref[c0:c0 + ff_chunk, :])
    xn = x + gt2 * acc
    if final:
        xn = _rms_norm(xn, fg_ref[...])
    o_ref[...] = xn


def _channel_mix(layer, x, mod, params, final, tm, ff_chunk=FFN_CHUNK):
    batch, seq, d = x.shape
    tile = pl.BlockSpec((None, tm, d), lambda b, j: (b, j, 0))
    operands = [x, mod, params["norm2_g"], params["w_ff1"], params["w_ff2"]]
    in_specs = [tile, pl.BlockSpec((None, 1, N_MOD * d), lambda b, j: (b, 0, 0)),
                _const_spec((1, d), layer), _const_spec(params["w_ff1"].shape[1:], layer),
                _const_spec(params["w_ff2"].shape[1:], layer)]
    if final:
        operands.append(params["final_g"])
        in_specs.append(_const_spec((1, d)))
    return pl.pallas_call(
        functools.partial(_ffn_kernel, final, ff_chunk),
        grid=(batch, seq // tm),
        in_specs=in_specs,
        out_specs=tile,
        out_shape=jax.ShapeDtypeStruct((batch, seq, d), F32),
        compiler_params=pltpu.CompilerParams(
            dimension_semantics=("arbitrary", "arbitrary"),
            vmem_limit_bytes=VMEM_LIMIT_BYTES),
    )(*operands)


def kernel(x, c, w_ada, b_ada, norm1_g, norm2_g, w_in, mu_shift, w0_decay, w2_decay, a0, w2_aaa, w2_gate, k_k, k_a, r_k, gn_w, gn_b, v0_res, w1_res, w2_res, ln_gmlp_w, ln_gmlp_b, w_spatial, b_spatial, w_br_rwkv, w_br_gmlp, w_out, w_ff1, w_ff2, final_g):
    depth = w_in.shape[0]
    batch, seq, d = x.shape
    ts = min(MIX_TOKENS, seq)
    tm = min(FFN_TOKENS, seq)
    row = lambda t: t.reshape(t.shape[0], 1, -1)
    zeros = lambda *s: jnp.zeros(s, F32)
    w2_wa = jnp.concatenate([
        jnp.concatenate([w2_decay, zeros(depth, DECAY_LORA, RWKV_WIDTH)], axis=2),
        jnp.concatenate([zeros(depth, AAA_LORA, RWKV_WIDTH), w2_aaa], axis=2)], axis=1)
    pad = LANES - VRES_LORA
    head_id = jnp.arange(MXU_WIDTH) // HEAD_DIM
    params = dict(
        norm1_g=row(norm1_g), norm2_g=row(norm2_g), w_in=w_in.astype(BF16), mu_shift=row(mu_shift),
        w0_decay=row(w0_decay), w2_wa=w2_wa.astype(BF16), a0=row(a0), w2_gate=w2_gate.astype(BF16),
        k_k=row(k_k), k_a=row(k_a), r_k=row(r_k), gn_w=row(gn_w), gn_b=row(gn_b),
        v0_res=row(v0_res),
        w1_res=jnp.pad(w1_res, ((0, 0), (0, 0), (0, pad))).astype(BF16),
        w2_res=jnp.pad(w2_res, ((0, 0), (0, pad), (0, 0))).astype(BF16),
        ln_gmlp_w=row(ln_gmlp_w), ln_gmlp_b=row(ln_gmlp_b), w_spatial=w_spatial,
        b_spatial_t=jnp.swapaxes(b_spatial, 1, 2),
        w_br_rwkv=w_br_rwkv.astype(BF16), w_br_gmlp=w_br_gmlp.astype(BF16),
        w_out=w_out.astype(BF16), w_ff1=w_ff1.astype(BF16), w_ff2=w_ff2.astype(BF16),
        final_g=final_g.reshape(1, d),
        seg_ones=(head_id[:, None] == head_id[None, :]).astype(BF16),
    )
    mod = _modulation(c, w_ada, b_ada)
    v_first = None
    for layer in range(depth):
        mod_l = mod[layer].reshape(batch, 1, N_MOD * d)
        if layer == 0:
            x, v_first = _token_mix(layer, x, mod_l, None, params, ts)
        else:
            (x,) = _token_mix(layer, x, mod_l, v_first, params, ts)
        x = _channel_mix(layer, x, mod_l, params, layer == depth - 1, tm)
    return x
```

```python
import functools

import jax
import jax.numpy as jnp
from jax import lax
from jax.experimental import pallas as pl
from jax.experimental.pallas import tpu as pltpu

F32 = jnp.float32
BF16 = jnp.bfloat16

RWKV_HEADS = 8
HEAD_DIM = 64
RWKV_WIDTH = RWKV_HEADS * HEAD_DIM
DECAY_LORA = 64
AAA_LORA = 64
GATE_LORA = 128
VRES_LORA = 32
GMLP_BLOCK = 128
GMLP_GROUPS = 4
GMLP_WIDTH = 512
STREAM_CHUNK = 64
N_MOD = 6
RMS_EPS = 1e-6
LN_EPS = 1e-5
GN_EPS = 64e-5
KK_NORM_FLOOR = 1e-12
EXACT_TERMS = 2
MIX_TOKENS = 512
FFN_TOKENS = 512
FFN_CHUNK = 1024

LANES = 128
MXU_WIDTH = 256
HEADS_PER_GROUP = MXU_WIDTH // HEAD_DIM
N_GROUPS = RWKV_HEADS // HEADS_PER_GROUP
SCAN_CHUNK = 64
ROW_RANGES = 2
SCAN_LEAD = 14
INV_BASE = 8
SIDE_COLS_PER_STEP = 3 * MXU_WIDTH
VMEM_LIMIT_BYTES =56 * 1024 * 1024

COL_LORA = 3 * RWKV_WIDTH
COL_GLORA = COL_LORA + DECAY_LORA + AAA_LORA
RWKV_COLS = COL_GLORA + GATE_LORA
COL_GATES = RWKV_COLS + 2 * GMLP_WIDTH


def _mm(a, b):
    return jnp.dot(a.astype(BF16), b.astype(BF16), preferred_element_type=F32)


def _split_bf16(x, terms):
    parts = []
    for _ in range(terms):
        p = x.astype(BF16)
        parts.append(p)
        x = x - p.astype(F32)
    return parts


def _head_sums(x, seg):
    w = seg.shape[0]
    xb = x.astype(BF16)
    return jnp.concatenate(
        [jnp.dot(xb[:, c0:c0 + w], seg, preferred_element_type=F32)
         for c0 in range(0, x.shape[1], w)], axis=1)


def _mm_exact_rhs(a, m01, terms):
    out = None
    for p in _split_bf16(a, terms):
        t = jnp.dot(p, m01, preferred_element_type=F32)
        out = t if out is None else out + t
    return out


def _mm_exact_lhs(m01, b, terms):
    out = None
    for p in _split_bf16(b, terms):
        t = jnp.dot(m01, p, preferred_element_type=F32)
        out = t if out is None else out + t
    return out


def _gelu_exact(x):
    return 0.5 * x * (1.0 + lax.erf(x * (2.0 ** -0.5)))


def _rms_norm(x, gain):
    ms = jnp.mean(x * x, axis=-1, keepdims=True)
    return (x * lax.rsqrt(ms + RMS_EPS)) * gain


def _mod_kernel(c_ref, w_ref, b_ref, o_ref):
    c = c_ref[...]
    c_act = c * jax.nn.sigmoid(c)
    o_ref[...] = _mm(c_act, w_ref[...]) + b_ref[...]


def _modulation(c, w_ada, b_ada):
    depth, d, _ = w_ada.shape
    batch = c.shape[0]
    return pl.pallas_call(
        _mod_kernel,
        grid=(depth, N_MOD),
        in_specs=[
            pl.BlockSpec((batch, d), lambda l, j: (0, 0)),
            pl.BlockSpec((None, d, d), lambda l, j: (l, 0, j)),
            pl.BlockSpec((None, 1, d), lambda l, j: (l, 0, j)),
        ],
        out_specs=pl.BlockSpec((None, batch, d), lambda l, j: (l, 0, j)),
        out_shape=jax.ShapeDtypeStruct((depth, batch, N_MOD * d), F32),
        compiler_params=pltpu.CompilerParams(
            dimension_semantics=("arbitrary", "arbitrary"),
            vmem_limit_bytes=VMEM_LIMIT_BYTES),
    )(c, w_ada, b_ada.reshape(depth, 1, N_MOD * d))


def _scan_tile(r, lw, k, v, a, b, seg, st_ref):
    ts = r.shape[0]
    n_chunks = ts // SCAN_CHUNK
    w = MXU_WIDTH
    ti = lax.broadcasted_iota(jnp.int32, (ts, ts), 0)
    tj = lax.broadcasted_iota(jnp.int32, (ts, ts), 1)
    tri_blk = ((ti // SCAN_CHUNK == tj // SCAN_CHUNK) & (ti >= tj)).astype(BF16)
    t_row = lax.broadcasted_iota(jnp.int32, (SCAN_CHUNK, MXU_WIDTH), 0)
    g_lane = lax.broadcasted_iota(jnp.int32, (SCAN_CHUNK, MXU_WIDTH), 1)
    t_col = g_lane % HEAD_DIM
    strict = (t_row > t_col).astype(F32)
    incl = (t_row >= t_col).astype(F32)
    eye = (t_row == t_col).astype(F32)
    head_of_lane = g_lane[0:1, :] // HEAD_DIM
    head_masks = [head_of_lane == hh for hh in range(HEADS_PER_GROUP)]

    def bd(x):
        xb = x.astype(BF16)
        return jnp.concatenate([jnp.where(m, xb, jnp.zeros_like(xb)) for m in head_masks], axis=0)

    def dot(x, wgt):
        return jnp.dot(x.astype(BF16), wgt, preferred_element_type=F32)

    def dot_nt(x, wgt):
        return lax.dot_general(x.astype(BF16), wgt, (((1,), (1,)), ((), ())),
                               preferred_element_type=F32)

    def part(z, c, q):
        return z[c * SCAN_CHUNK:(c + 1) * SCAN_CHUNK, q * w:(q + 1) * w]

    cl = _mm_exact_lhs(tri_blk, lw, EXACT_TERMS)
    e_ncl = jnp.exp(-cl)
    a_t = (a * jnp.exp(cl - lw)).astype(BF16)
    r_t = r * jnp.exp(cl)
    b_t = (b * e_ncl).astype(BF16)
    k_t = (k * e_ncl).astype(BF16)
    v = v.astype(BF16)
    w_diag = [jnp.concatenate([eye, eye], axis=1)
              * jnp.exp(cl[(c + 1) * SCAN_CHUNK - 1:(c + 1) * SCAN_CHUNK, :]) for c in range(n_chunks)]
    w_diag = jnp.concatenate(w_diag, axis=0)
    w_col = [_mm_exact_rhs(w_diag[:, q * w:(q + 1) * w], seg, EXACT_TERMS)
             for q in range(N_GROUPS)]

    insts = [(c, q) for c in range(n_chunks) for q in range(N_GROUPS)]
    a_i = {i: part(a_t, *i) for i in insts}
    r_i = {i: part(r_t, *i) for i in insts}
    v_i = {i: part(v, *i) for i in insts}
    w_i = {(c, q): w_col[q][c * SCAN_CHUNK:(c + 1) * SCAN_CHUNK, :] for c, q in insts}
    eye_b = eye.astype(BF16)
    gg = {i: dot_nt(jnp.concatenate([a_i[i], r_i[i].astype(BF16), eye_b], axis=0),
                    jnp.concatenate([bd(part(b_t, *i)), bd(part(k_t, *i))], axis=0))
          for i in insts}
    yield
    l_ab = {i: gg[i][:SCAN_CHUNK, :w] * strict for i in insts}
    rows_v = {i: jnp.concatenate([gg[i][:SCAN_CHUNK, w:] * strict,
                                  gg[i][SCAN_CHUNK:2 * SCAN_CHUNK, w:] * incl,
                                  gg[i][2 * SCAN_CHUNK:, w:] * w_i[i]], axis=0).astype(BF16)
              for i in insts}
    rows_t = {i: jnp.concatenate([gg[i][SCAN_CHUNK:2 * SCAN_CHUNK, :w] * incl,
                                  gg[i][2 * SCAN_CHUNK:, :w] * w_i[i]], axis=0).astype(BF16)
              for i in insts}
    same_block = lambda size: (t_row // size) == (t_col // size)
    l_b = {i: l_ab[i].astype(BF16) for i in insts}
    zero_b = jnp.zeros((SCAN_CHUNK, w), BF16)
    l_d = {i: jnp.where(same_block(INV_BASE), l_b[i], zero_b) for i in insts}
    t_inv = {i: eye + l_d[i] for i in insts}
    pw = {i: dot(l_d[i], bd(l_d[i])).astype(BF16) for i in insts}
    xv = {i: dot(rows_v[i], bd(v_i[i])) for i in insts}
    n = 4
    while n < INV_BASE:
        yield
        for i in insts:
            both = dot(jnp.concatenate([pw[i], t_inv[i].astype(BF16)], axis=0), bd(pw[i]))
            pw[i] = both[:SCAN_CHUNK].astype(BF16)
            t_inv[i] = t_inv[i] + both[SCAN_CHUNK:]
        n *= 2
    yield
    for i in insts:
        t_inv[i] = (t_inv[i] + dot(t_inv[i], bd(pw[i]))).astype(BF16)
    size = INV_BASE
    while size < SCAN_CHUNK:
        lower_left = same_block(2 * size) & ~same_block(size)
        yield
        tl = {i: dot(t_inv[i], bd(jnp.where(lower_left, l_b[i], zero_b))) for i in insts}
        yield
        for i in insts:
            t_inv[i] = t_inv[i] + dot(tl[i], bd(t_inv[i])).astype(BF16)
        size *= 2
    yield
    xt = {i: dot(rows_t[i], bd(t_inv[i])) for i in insts}
    yield
    xpq = {i: dot(xt[i], jnp.concatenate([bd(a_i[i]), bd(xv[i][:SCAN_CHUNK])], axis=1))
           for i in insts}
    yield
    r_phi = {i: jnp.concatenate([r_i[i] + xpq[i][:SCAN_CHUNK, :w], xpq[i][SCAN_CHUNK:, :w]], axis=0)
             for i in insts}
    y_loc = {i: xpq[i][:SCAN_CHUNK, w:] + xv[i][SCAN_CHUNK:2 * SCAN_CHUNK] for i in insts}
    psi_t = {i: xpq[i][SCAN_CHUNK:, w:] + xv[i][2 * SCAN_CHUNK:] for i in insts}
    y_rows = []
    s_t = [st_ref[q] for q in range(N_GROUPS)]
    for c in range(n_chunks):
        y_cols = []
        for q in range(N_GROUPS):
            i = (c, q)
            both = dot(r_phi[i], bd(s_t[q]))
            y_cols.append(both[:SCAN_CHUNK] + y_loc[i])
            s_t[q] = w_i[i] * s_t[q] + both[SCAN_CHUNK:] + psi_t[i]
        y_rows.append(jnp.concatenate(y_cols, axis=1))
        yield
    for q in range(N_GROUPS):
        st_ref[q] = s_t[q]
    return jnp.concatenate(y_rows, axis=0)


def _gmlp_and_gates(h, w_in_ref, lnw_ref, lnb_ref, ws_ref, bst_ref, wbr_gm_ref, ts):
    cw = SIDE_COLS_PER_STEP
    p_cols = []
    for c0 in range(RWKV_COLS, w_in_ref.shape[1], cw):
        p_cols.append(jnp.dot(h, w_in_ref[:, c0:c0 + cw], preferred_element_type=F32))
        yield
    p_side = jnp.concatenate(p_cols, axis=1)
    z_cols = []
    for c0 in range(0, 2 * GMLP_WIDTH, MXU_WIDTH):
        z_cols.append(_gelu_exact(p_side[:, c0:c0 + MXU_WIDTH]))
        yield
    half = len(z_cols) // 2
    u_g = jnp.concatenate(z_cols[:half], axis=1)
    v_g = jnp.concatenate(z_cols[half:], axis=1)
    mu_g = jnp.mean(v_g, axis=-1, keepdims=True)
    vc_g = v_g - mu_g
    var_g = jnp.mean(vc_g * vc_g, axis=-1, keepdims=True)
    v_g = vc_g * lax.rsqrt(var_g + LN_EPS) * lnw_ref[...] + lnb_ref[...]
    yield
    bi = lax.broadcasted_iota(jnp.int32, (GMLP_BLOCK, GMLP_BLOCK), 0)
    bj = lax.broadcasted_iota(jnp.int32, (GMLP_BLOCK, GMLP_BLOCK), 1)
    causal = (bi // STREAM_CHUNK) >= (bj // STREAM_CHUNK)
    gd = GMLP_WIDTH // GMLP_GROUPS
    bst = bst_ref[...]
    sv_cols = []
    for gi in range(GMLP_GROUPS):
        w_g = jnp.where(causal, ws_ref[gi], 0.0)
        bias = bst[:, gi:gi + 1]
        v_blocks = jnp.concatenate(
            [v_g[nb * GMLP_BLOCK:(nb + 1) * GMLP_BLOCK, gi * gd:(gi + 1) * gd]
             for nb in range(ts // GMLP_BLOCK)], axis=1)
        sv = _mm(w_g, v_blocks) + bias
        sv_cols.append(jnp.concatenate(
            [sv[:, nb * gd:(nb + 1) * gd] for nb in range(ts // GMLP_BLOCK)], axis=0))
    yield
    y_gm = u_g * jnp.concatenate(sv_cols, axis=1)
    yield
    gate_cols = []
    for c0 in range(2 * GMLP_WIDTH, p_side.shape[1], GMLP_WIDTH):
        gate_cols.append(jax.nn.sigmoid(p_side[:, c0:c0 + GMLP_WIDTH]))
        yield
    gates = jnp.concatenate(gate_cols, axis=1)
    d = gates.shape[1] // 2
    yield
    br_cols = []
    steps_per_block = max(1, (ts // SCAN_CHUNK) // (d // MXU_WIDTH))
    for c0 in range(0, d, MXU_WIDTH):
        br_cols.append(_mm(y_gm, wbr_gm_ref[:, c0:c0 + MXU_WIDTH]))
        for _ in range(steps_per_block):
            yield
    return gates[:, 0:d], gates[:, d:] * jnp.concatenate(br_cols, axis=1)


def _interleave(*gens):
    out = [None] * len(gens)
    live = list(gens)
    while any(g is not None for g in live):
        for n, g in enumerate(live):
            if g is not None:
                try:
                    next(g)
                except StopIteration as stop:
                    out[n], live[n] = stop.value, None
    return out


def _mix_kernel(has_vres, ts, *refs):
    it = iter(refs)
    x_ref, mod_ref = next(it), next(it)
    vfirst_ref = next(it) if has_vres else None
    (g1_ref, w_in_ref, mu_ref, w0_ref, w2wa_ref, a0_ref, w2g_ref, kk_ref, ka_ref, rk_ref,
     gnw_ref, gnb_ref) = (next(it) for _ in range(12))
    if has_vres:
        v0_ref, w1r_ref, w2r_ref = next(it), next(it), next(it)
    (lnw_ref, lnb_ref, ws_ref, bst_ref, wbr_rw_ref, wbr_gm_ref, w_out_ref, seg_ref) = (
        next(it) for _ in range(8))
    xo_ref = next(it)
    vo_ref = None if has_vres else next(it)
    st_ref, carry_ref = next(it), next(it)

    d = x_ref.shape[-1]
    j = pl.program_id(1)

    @pl.when(j == 0)
    def _():
        st_ref[...] = jnp.zeros_like(st_ref)
        carry_ref[...] = jnp.zeros_like(carry_ref)

    x = x_ref[...]
    mod = mod_ref[...]
    sh1, sc1, gt1 = mod[:, 0:d], mod[:, d:2 * d], mod[:, 2 * d:3 * d]
    h = (_rms_norm(x, g1_ref[...]) * (1.0 + sc1) + sh1).astype(BF16)

    seg = seg_ref[...]

    def rwkv_branch(r0, rows, delay):
        for _ in range(delay):
            yield
        p_rw = jnp.dot(h[r0:r0 + rows, :], w_in_ref[:, 0:RWKV_COLS], preferred_element_type=F32)
        yield
        row = lax.broadcasted_iota(jnp.int32, (rows, 1), 0)
        prev = jnp.where(row == 0, carry_ref[...], pltpu.roll(p_rw, 1, 0))
        carry_ref[...] = p_rw[rows - 1:rows, :]
        xs = p_rw + (prev - p_rw) * mu_ref[...]
        r = xs[:, 0:RWKV_WIDTH]
        k = xs[:, RWKV_WIDTH:2 * RWKV_WIDTH]
        v = xs[:, 2 * RWKV_WIDTH:3 * RWKV_WIDTH]
        lo = xs[:, COL_LORA:COL_GLORA]
        g_lo = xs[:, COL_GLORA:RWKV_COLS]
        yield
        lane = lax.broadcasted_iota(jnp.int32, (1, LANES), 1)
        lo_act = jnp.where(lane < DECAY_LORA, jnp.tanh(lo), lo)
        wa = _mm(lo_act, w2wa_ref[...])
        w_log = -jax.nn.softplus(-(w0_ref[...] + wa[:, 0:RWKV_WIDTH])) - 0.5
        a = jax.nn.sigmoid(a0_ref[...] + wa[:, RWKV_WIDTH:])
        gate = _mm(jax.nn.sigmoid(g_lo), w2g_ref[...])
        yield
        if has_vres:
            vmix = jax.nn.sigmoid(v0_ref[...] + _mm(_mm(v, w1r_ref[...]), w2r_ref[...]))
            v = v + (vfirst_ref[r0:r0 + rows, :] - v) * vmix
        else:
            vo_ref[r0:r0 + rows, :] = v
        kk = k * kk_ref[...]
        kk = kk * lax.rsqrt(jnp.maximum(_head_sums(kk * kk, seg), KK_NORM_FLOOR ** 2))
        k = k * (1.0 + (a - 1.0) * ka_ref[...])
        bonus = _head_sums(r * k * rk_ref[...], seg) * v
        yield
        y = yield from _scan_tile(r, -jnp.exp(w_log), k, v, -kk, kk * a, seg, st_ref)
        return y, bonus, gate

    rows = ts // ROW_RANGES
    *ranges, (gate_rw, gm_term) = _interleave(
        *[rwkv_branch(n * rows, rows, n * SCAN_LEAD) for n in range(ROW_RANGES)],
        _gmlp_and_gates(h, w_in_ref, lnw_ref, lnb_ref, ws_ref, bst_ref, wbr_gm_ref, ts))
    y, bonus, gate = (jnp.concatenate(parts, axis=0) for parts in zip(*ranges))
    mean = _head_sums(y, seg) * (1.0 / HEAD_DIM)
    yc = y - mean
    var = _head_sums(yc * yc, seg) * (1.0 / HEAD_DIM)
    y = yc * lax.rsqrt(var + GN_EPS) * gnw_ref[...] + gnb_ref[...]
    y_rw = (y + bonus) * gate

    merged = gate_rw * _mm(y_rw, wbr_rw_ref[...]) + gm_term
    xo_ref[...] = x + gt1 * _mm(merged, w_out_ref[...])


def _const_spec(shape, layer=None):
    if layer is None:
        return pl.BlockSpec(shape, lambda b, j: (0,) * len(shape), pipeline_mode=pl.Buffered(1))
    return pl.BlockSpec((None,) + shape, lambda b, j: (layer,) + (0,) * len(shape),
                        pipeline_mode=pl.Buffered(1))


def _token_mix(layer, x, mod, v_first, params, ts):
    batch, seq, d = x.shape
    has_vres = layer > 0
    tile = lambda w: pl.BlockSpec((None, ts, w), lambda b, j: (b, j, 0))
    row512 = _const_spec((1, RWKV_WIDTH), layer)
    p = params
    operands = [x, mod]
    in_specs = [tile(d), pl.BlockSpec((None, 1, N_MOD * d), lambda b, j: (b, 0, 0))]
    if has_vres:
        operands.append(v_first)
        in_specs.append(tile(RWKV_WIDTH))
    operands += [p["norm1_g"], p["w_in"], p["mu_shift"], p["w0_decay"], p["w2_wa"], p["a0"],
                 p["w2_gate"], p["k_k"], p["k_a"], p["r_k"], p["gn_w"], p["gn_b"]]
    in_specs += [_const_spec((1, d), layer), _const_spec(p["w_in"].shape[1:], layer),
                 _const_spec((1, RWKV_COLS), layer), row512,
                 _const_spec(p["w2_wa"].shape[1:], layer), row512,
                 _const_spec(p["w2_gate"].shape[1:], layer), row512, row512, row512, row512, row512]
    if has_vres:
        operands += [p["v0_res"], p["w1_res"], p["w2_res"]]
        in_specs += [_const_spec((1, RWKV_WIDTH), layer - 1),
                     _const_spec(p["w1_res"].shape[1:], layer - 1),
                     _const_spec(p["w2_res"].shape[1:], layer - 1)]
    operands += [p["ln_gmlp_w"], p["ln_gmlp_b"], p["w_spatial"], p["b_spatial_t"],
                 p["w_br_rwkv"], p["w_br_gmlp"], p["w_out"], p["seg_ones"]]
    in_specs += [_const_spec((1, GMLP_WIDTH), layer), _const_spec((1, GMLP_WIDTH), layer),
                 _const_spec(p["w_spatial"].shape[1:], layer),
                 _const_spec(p["b_spatial_t"].shape[1:], layer),
                 _const_spec(p["w_br_rwkv"].shape[1:], layer),
                 _const_spec(p["w_br_gmlp"].shape[1:], layer),
                 _const_spec(p["w_out"].shape[1:], layer),
                 _const_spec(p["seg_ones"].shape)]
    out_shape = [jax.ShapeDtypeStruct((batch, seq, d), F32)]
    out_specs = [tile(d)]
    if not has_vres:
        out_shape.append(jax.ShapeDtypeStruct((batch, seq, RWKV_WIDTH), F32))
        out_specs.append(tile(RWKV_WIDTH))
    scratch = [pltpu.VMEM((N_GROUPS, HEAD_DIM, MXU_WIDTH), F32), pltpu.VMEM((1, RWKV_COLS), F32)]
    outs = pl.pallas_call(
        functools.partial(_mix_kernel, has_vres, ts),
        grid=(batch, seq // ts),
        in_specs=in_specs,
        out_specs=out_specs,
        out_shape=out_shape,
        scratch_shapes=scratch,
        compiler_params=pltpu.CompilerParams(
            dimension_semantics=("arbitrary", "arbitrary"),
            vmem_limit_bytes=VMEM_LIMIT_BYTES),
    )(*operands)
    return outs if has_vres else (outs[0], outs[1])


def _ffn_kernel(final, ff_chunk, *refs):
    if final:
        x_ref, mod_ref, g2_ref, w1_ref, w2_ref, fg_ref, o_ref = refs
    else:
        x_ref, mod_ref, g2_ref, w1_ref, w2_ref, o_ref = refs
    d = x_ref.shape[-1]
    x = x_ref[...]
    mod = mod_ref[...]
    sh2, sc2, gt2 = mod[:, 3 * d:4 * d], mod[:, 4 * d:5 * d], mod[:, 5 * d:6 * d]
    h = (_rms_norm(x, g2_ref[...]) * (1.0 + sc2) + sh2).astype(BF16)
    acc = jnp.zeros_like(x)
    for c0 in range(0, w1_ref.shape[1], ff_chunk):
        t = jnp.maximum(jnp.dot(h, w1_ref[:, c0:c0 + ff_chunk], preferred_element_type=F32), 0.0)
        acc = acc + _mm(t * t, w2_ref[c0:c0 + ff_chunk, :])
    xn = x + gt2 * acc
    if final:
        xn = _rms_norm(xn, fg_ref[...])
    o_ref[...] = xn


def _channel_mix(layer, x, mod, params, final, tm, ff_chunk=FFN_CHUNK):
    batch, seq, d = x.shape
    tile = pl.BlockSpec((None, tm, d), lambda b, j: (b, j, 0))
    operands = [x, mod, params["norm2_g"], params["w_ff1"], params["w_ff2"]]
    in_specs = [tile, pl.BlockSpec((None, 1, N_MOD * d), lambda b, j: (b, 0, 0)),
                _const_spec((1, d), layer), _const_spec(params["w_ff1"].shape[1:], layer),
                _const_spec(params["w_ff2"].shape[1:], layer)]
    if final:
        operands.append(params["final_g"])
        in_specs.append(_const_spec((1, d)))
    return pl.pallas_call(
        functools.partial(_ffn_kernel, final, ff_chunk),
        grid=(batch, seq // tm),
        in_specs=in_specs,
        out_specs=tile,
        out_shape=jax.ShapeDtypeStruct((batch, seq, d), F32),
        compiler_params=pltpu.CompilerParams(
            dimension_semantics=("arbitrary", "arbitrary"),
            vmem_limit_bytes=VMEM_LIMIT_BYTES),
    )(*operands)


def kernel(x, c, w_ada, b_ada, norm1_g, norm2_g, w_in, mu_shift, w0_decay, w2_decay, a0, w2_aaa, w2_gate, k_k, k_a, r_k, gn_w, gn_b, v0_res, w1_res, w2_res, ln_gmlp_w, ln_gmlp_b, w_spatial, b_spatial, w_br_rwkv, w_br_gmlp, w_out, w_ff1, w_ff2, final_g):
    depth = w_in.shape[0]
    batch, seq, d = x.shape
    ts = min(MIX_TOKENS, seq)
    tm = min(FFN_TOKENS, seq)
    row = lambda t: t.reshape(t.shape[0], 1, -1)
    zeros = lambda *s: jnp.zeros(s, F32)
    w2_wa = jnp.concatenate([
        jnp.concatenate([w2_decay, zeros(depth, DECAY_LORA, RWKV_WIDTH)], axis=2),
        jnp.concatenate([zeros(depth, AAA_LORA, RWKV_WIDTH), w2_aaa], axis=2)], axis=1)
    pad = LANES - VRES_LORA
    head_id = jnp.arange(MXU_WIDTH) // HEAD_DIM
    params = dict(
        norm1_g=row(norm1_g), norm2_g=row(norm2_g), w_in=w_in.astype(BF16), mu_shift=row(mu_shift),
        w0_decay=row(w0_decay), w2_wa=w2_wa.astype(BF16), a0=row(a0), w2_gate=w2_gate.astype(BF16),
        k_k=row(k_k), k_a=row(k_a), r_k=row(r_k), gn_w=row(gn_w), gn_b=row(gn_b),
        v0_res=row(v0_res),
        w1_res=jnp.pad(w1_res, ((0, 0), (0, 0), (0, pad))).astype(BF16),
        w2_res=jnp.pad(w2_res, ((0, 0), (0, pad), (0, 0))).astype(BF16),
        ln_gmlp_w=row(ln_gmlp_w), ln_gmlp_b=row(ln_gmlp_b), w_spatial=w_spatial,
        b_spatial_t=jnp.swapaxes(b_spatial, 1, 2),
        w_br_rwkv=w_br_rwkv.astype(BF16), w_br_gmlp=w_br_gmlp.astype(BF16),
        w_out=w_out.astype(BF16), w_ff1=w_ff1.astype(BF16), w_ff2=w_ff2.astype(BF16),
        final_g=final_g.reshape(1, d),
        seg_ones=(head_id[:, None] == head_id[None, :]).astype(BF16),
    )
    mod = _modulation(c, w_ada, b_ada)
    v_first = None
    for layer in range(depth):
        mod_l = mod[layer].reshape(batch, 1, N_MOD * d)
        if layer == 0:
            x, v_first = _token_mix(layer, x, mod_l, None, params, ts)
        else:
            (x,) = _token_mix(layer, x, mod_l, v_first, params, ts)
        x = _channel_mix(layer, x, mod_l, params, layer == depth - 1, tm)
    return x
```

```python
import functools

import jax
import jax.numpy as jnp
from jax import lax
from jax.experimental import pallas as pl
from jax.experimental.pallas import tpu as pltpu

F32 = jnp.float32
BF16 = jnp.bfloat16

RWKV_HEADS = 8
HEAD_DIM = 64
RWKV_WIDTH = RWKV_HEADS * HEAD_DIM
DECAY_LORA = 64
AAA_LORA = 64
GATE_LORA = 128
VRES_LORA = 32
GMLP_BLOCK = 128
GMLP_GROUPS = 4
GMLP_WIDTH = 512
STREAM_CHUNK = 64
N_MOD = 6
RMS_EPS = 1e-6
LN_EPS = 1e-5
GN_EPS = 64e-5
KK_NORM_FLOOR = 1e-12
EXACT_TERMS = 2
MIX_TOKENS = 512
FFN_TOKENS = 512
FFN_CHUNK = 1024

LANES = 128
MXU_WIDTH = 256
HEADS_PER_GROUP = MXU_WIDTH // HEAD_DIM
N_GROUPS = RWKV_HEADS // HEADS_PER_GROUP
SCAN_CHUNK = 64
ROW_RANGES = 2
SCAN_LEAD = 14
INV_BASE = 8
SIDE_COLS_PER_STEP = 3 * MXU_WIDTH
VMEM_LIMIT_BYTES =56 * 1024 * 1024

COL_LORA = 3 * RWKV_WIDTH
COL_GLORA = COL_LORA + DECAY_LORA + AAA_LORA
RWKV_COLS = COL_GLORA + GATE_LORA
COL_GATES = RWKV_COLS + 2 * GMLP_WIDTH


def _mm(a, b):
    return jnp.dot(a.astype(BF16), b.astype(BF16), preferred_element_type=F32)


def _split_bf16(x, terms):
    parts = []
    for _ in range(terms):
        p = x.astype(BF16)
        parts.append(p)
        x = x - p.astype(F32)
    return parts


def _head_sums(x, seg):
    w = seg.shape[0]
    xb = x.astype(BF16)
    return jnp.concatenate(
        [jnp.dot(xb[:, c0:c0 + w], seg, preferred_element_type=F32)
         for c0 in range(0, x.shape[1], w)], axis=1)


def _mm_exact_rhs(a, m01, terms):
    out = None
    for p in _split_bf16(a, terms):
        t = jnp.dot(p, m01, preferred_element_type=F32)
        out = t if out is None else out + t
    return out


def _mm_exact_lhs(m01, b, terms):
    out = None
    for p in _split_bf16(b, terms):
        t = jnp.dot(m01, p, preferred_element_type=F32)
        out = t if out is None else out + t
    return out


def _gelu_exact(x):
    return 0.5 * x * (1.0 + lax.erf(x * (2.0 ** -0.5)))


def _rms_norm(x, gain):
    ms = jnp.mean(x * x, axis=-1, keepdims=True)
    return (x * lax.rsqrt(ms + RMS_EPS)) * gain


def _mod_kernel(c_ref, w_ref, b_ref, o_ref):
    c = c_ref[...]
    c_act = c * jax.nn.sigmoid(c)
    o_ref[...] = _mm(c_act, w_ref[...]) + b_ref[...]


def _modulation(c, w_ada, b_ada):
    depth, d, _ = w_ada.shape
    batch = c.shape[0]
    return pl.pallas_call(
        _mod_kernel,
        grid=(depth, N_MOD),
        in_specs=[
            pl.BlockSpec((batch, d), lambda l, j: (0, 0)),
            pl.BlockSpec((None, d, d), lambda l, j: (l, 0, j)),
            pl.BlockSpec((None, 1, d), lambda l, j: (l, 0, j)),
        ],
        out_specs=pl.BlockSpec((None, batch, d), lambda l, j: (l, 0, j)),
        out_shape=jax.ShapeDtypeStruct((depth, batch, N_MOD * d), F32),
        compiler_params=pltpu.CompilerParams(
            dimension_semantics=("arbitrary", "arbitrary"),
            vmem_limit_bytes=VMEM_LIMIT_BYTES),
    )(c, w_ada, b_ada.reshape(depth, 1, N_MOD * d))


def _scan_tile(r, lw, k, v, a, b, seg, st_ref):
    ts = r.shape[0]
    n_chunks = ts // SCAN_CHUNK
    w = MXU_WIDTH
    ti = lax.broadcasted_iota(jnp.int32, (ts, ts), 0)
    tj = lax.broadcasted_iota(jnp.int32, (ts, ts), 1)
    tri_blk = ((ti // SCAN_CHUNK == tj // SCAN_CHUNK) & (ti >= tj)).astype(BF16)
    t_row = lax.broadcasted_iota(jnp.int32, (SCAN_CHUNK, MXU_WIDTH), 0)
    g_lane = lax.broadcasted_iota(jnp.int32, (SCAN_CHUNK, MXU_WIDTH), 1)
    t_col = g_lane % HEAD_DIM
    strict = (t_row > t_col).astype(F32)
    incl = (t_row >= t_col).astype(F32)
    eye = (t_row == t_col).astype(F32)
    head_of_lane = g_lane[0:1, :] // HEAD_DIM
    head_masks = [head_of_lane == hh for hh in range(HEADS_PER_GROUP)]

    def bd(x):
        xb = x.astype(BF16)
        return jnp.concatenate([jnp.where(m, xb, jnp.zeros_like(xb)) for m in head_masks], axis=0)

    def dot(x, wgt):
        return jnp.dot(x.astype(BF16), wgt, preferred_element_type=F32)

    def dot_nt(x, wgt):
        return lax.dot_general(x.astype(BF16), wgt, (((1,), (1,)), ((), ())),
                               preferred_element_type=F32)

    def part(z, c, q):
        return z[c * SCAN_CHUNK:(c + 1) * SCAN_CHUNK, q * w:(q + 1) * w]

    cl = _mm_exact_lhs(tri_blk, lw, EXACT_TERMS)
    e_ncl = jnp.exp(-cl)
    a_t = (a * jnp.exp(cl - lw)).astype(BF16)
    r_t = r * jnp.exp(cl)
    b_t = (b * e_ncl).astype(BF16)
    k_t = (k * e_ncl).astype(BF16)
    v = v.astype(BF16)
    w_diag = [jnp.concatenate([eye, eye], axis=1)
              * jnp.exp(cl[(c + 1) * SCAN_CHUNK - 1:(c + 1) * SCAN_CHUNK, :]) for c in range(n_chunks)]
    w_diag = jnp.concatenate(w_diag, axis=0)
    w_col = [_mm_exact_rhs(w_diag[:, q * w:(q + 1) * w], seg, EXACT_TERMS)
             for q in range(N_GROUPS)]

    insts = [(c, q) for c in range(n_chunks) for q in range(N_GROUPS)]
    a_i = {i: part(a_t, *i) for i in insts}
    r_i = {i: part(r_t, *i) for i in insts}
    v_i = {i: part(v, *i) for i in insts}
    w_i = {(c, q): w_col[q][c * SCAN_CHUNK:(c + 1) * SCAN_CHUNK, :] for c, q in insts}
    eye_b = eye.astype(BF16)
    gg = {i: dot_nt(jnp.concatenate([a_i[i], r_i[i].astype(BF16), eye_b], axis=0),
                    jnp.concatenate([bd(part(b_t, *i)), bd(part(k_t, *i))], axis=0))
          for i in insts}
    yield
    l_ab = {i: gg[i][:SCAN_CHUNK, :w] * strict for i in insts}
    rows_v = {i: jnp.concatenate([gg[i][:SCAN_CHUNK, w:] * strict,
                                  gg[i][SCAN_CHUNK:2 * SCAN_CHUNK, w:] * incl,
                                  gg[i][2 * SCAN_CHUNK:, w:] * w_i[i]], axis=0).astype(BF16)
              for i in insts}
    rows_t = {i: jnp.concatenate([gg[i][SCAN_CHUNK:2 * SCAN_CHUNK, :w] * incl,
                                  gg[i][2 * SCAN_CHUNK:, :w] * w_i[i]], axis=0).astype(BF16)
              for i in insts}
    same_block = lambda size: (t_row // size) == (t_col // size)
    l_b = {i: l_ab[i].astype(BF16) for i in insts}
    zero_b = jnp.zeros((SCAN_CHUNK, w), BF16)
    l_d = {i: jnp.where(same_block(INV_BASE), l_b[i], zero_b) for i in insts}
    t_inv = {i: eye + l_d[i] for i in insts}
    pw = {i: dot(l_d[i], bd(l_d[i])).astype(BF16) for i in insts}
    xv = {i: dot(rows_v[i], bd(v_i[i])) for i in insts}
    n = 4
    while n < INV_BASE:
        yield
        for i in insts:
            both = dot(jnp.concatenate([pw[i], t_inv[i].astype(BF16)], axis=0), bd(pw[i]))
            pw[i] = both[:SCAN_CHUNK].astype(BF16)
            t_inv[i] = t_inv[i] + both[SCAN_CHUNK:]
        n *= 2
    yield
    for i in insts:
        t_inv[i] = (t_inv[i] + dot(t_inv[i], bd(pw[i]))).astype(BF16)
    size = INV_BASE
    while size < SCAN_CHUNK:
        lower_left = same_block(2 * size) & ~same_block(size)
        yield
        tl = {i: dot(t_inv[i], bd(jnp.where(lower_left, l_b[i], zero_b))) for i in insts}
        yield
        for i in insts:
            t_inv[i] = t_inv[i] + dot(tl[i], bd(t_inv[i])).astype(BF16)
        size *= 2
    yield
    xt = {i: dot(rows_t[i], bd(t_inv[i])) for i in insts}
    yield
    xpq = {i: dot(xt[i], jnp.concatenate([bd(a_i[i]), bd(xv[i][:SCAN_CHUNK])], axis=1))
           for i in insts}
    yield
    r_phi = {i: jnp.concatenate([r_i[i] + xpq[i][:SCAN_CHUNK, :w], xpq[i][SCAN_CHUNK:, :w]], axis=0)
             for i in insts}
    y_loc = {i: xpq[i][:SCAN_CHUNK, w:] + xv[i][SCAN_CHUNK:2 * SCAN_CHUNK] for i in insts}
    psi_t = {i: xpq[i][SCAN_CHUNK:, w:] + xv[i][2 * SCAN_CHUNK:] for i in insts}
    y_rows = []
    s_t = [st_ref[q] for q in range(N_GROUPS)]
    for c in range(n_chunks):
        y_cols = []
        for q in range(N_GROUPS):
            i = (c, q)
            both = dot(r_phi[i], bd(s_t[q]))
            y_cols.append(both[:SCAN_CHUNK] + y_loc[i])
            s_t[q] = w_i[i] * s_t[q] + both[SCAN_CHUNK:] + psi_t[i]
        y_rows.append(jnp.concatenate(y_cols, axis=1))
        yield
    for q in range(N_GROUPS):
        st_ref[q] = s_t[q]
    return jnp.concatenate(y_rows, axis=0)


def _gmlp_and_gates(h, w_in_ref, lnw_ref, lnb_ref, ws_ref, bst_ref, wbr_gm_ref, ts, delay):
    for _ in range(delay):
        yield
    cw = SIDE_COLS_PER_STEP
    p_cols = []
    for c0 in range(RWKV_COLS, w_in_ref.shape[1], cw):
        p_cols.append(jnp.dot(h, w_in_ref[:, c0:c0 + cw], preferred_element_type=F32))
        yield
    p_side = jnp.concatenate(p_cols, axis=1)
    z_cols = []
    for c0 in range(0, 2 * GMLP_WIDTH, MXU_WIDTH):
        z_cols.append(_gelu_exact(p_side[:, c0:c0 + MXU_WIDTH]))
        yield
    half = len(z_cols) // 2
    u_g = jnp.concatenate(z_cols[:half], axis=1)
    v_g = jnp.concatenate(z_cols[half:], axis=1)
    mu_g = jnp.mean(v_g, axis=-1, keepdims=True)
    vc_g = v_g - mu_g
    var_g = jnp.mean(vc_g * vc_g, axis=-1, keepdims=True)
    v_g = vc_g * lax.rsqrt(var_g + LN_EPS) * lnw_ref[...] + lnb_ref[...]
    yield
    bi = lax.broadcasted_iota(jnp.int32, (GMLP_BLOCK, GMLP_BLOCK), 0)
    bj = lax.broadcasted_iota(jnp.int32, (GMLP_BLOCK, GMLP_BLOCK), 1)
    causal = (bi // STREAM_CHUNK) >= (bj // STREAM_CHUNK)
    gd = GMLP_WIDTH // GMLP_GROUPS
    bst = bst_ref[...]
    sv_cols = []
    for gi in range(GMLP_GROUPS):
        w_g = jnp.where(causal, ws_ref[gi], 0.0)
        bias = bst[:, gi:gi + 1]
        v_blocks = jnp.concatenate(
            [v_g[nb * GMLP_BLOCK:(nb + 1) * GMLP_BLOCK, gi * gd:(gi + 1) * gd]
             for nb in range(ts // GMLP_BLOCK)], axis=1)
        sv = _mm(w_g, v_blocks) + bias
        sv_cols.append(jnp.concatenate(
            [sv[:, nb * gd:(nb + 1) * gd] for nb in range(ts // GMLP_BLOCK)], axis=0))
    yield
    y_gm = u_g * jnp.concatenate(sv_cols, axis=1)
    yield
    gate_cols = []
    for c0 in range(2 * GMLP_WIDTH, p_side.shape[1], GMLP_WIDTH):
        gate_cols.append(jax.nn.sigmoid(p_side[:, c0:c0 + GMLP_WIDTH]))
        yield
    gates = jnp.concatenate(gate_cols, axis=1)
    d = gates.shape[1] // 2
    yield
    br_cols = []
    steps_per_block = max(1, (ts // SCAN_CHUNK) // (d // MXU_WIDTH))
    for c0 in range(0, d, MXU_WIDTH):
        br_cols.append(_mm(y_gm, wbr_gm_ref[:, c0:c0 + MXU_WIDTH]))
        for _ in range(steps_per_block):
            yield
    return gates[:, 0:d], gates[:, d:] * jnp.concatenate(br_cols, axis=1)


def _interleave(*gens):
    out = [None] * len(gens)
    live = list(gens)
    while any(g is not None for g in live):
        for n, g in enumerate(live):
            if g is not None:
                try:
                    next(g)
                except StopIteration as stop:
                    out[n], live[n] = stop.value, None
    return out


def _mix_kernel(has_vres, ts, *refs):
    it = iter(refs)
    x_ref, mod_ref = next(it), next(it)
    vfirst_ref = next(it) if has_vres else None
    (g1_ref, w_in_ref, mu_ref, w0_ref, w2wa_ref, a0_ref, w2g_ref, kk_ref, ka_ref, rk_ref,
     gnw_ref, gnb_ref) = (next(it) for _ in range(12))
    if has_vres:
        v0_ref, w1r_ref, w2r_ref = next(it), next(it), next(it)
    (lnw_ref, lnb_ref, ws_ref, bst_ref, wbr_rw_ref, wbr_gm_ref, w_out_ref, seg_ref) = (
        next(it) for _ in range(8))
    xo_ref = next(it)
    vo_ref = None if has_vres else next(it)
    st_ref, carry_ref = next(it), next(it)

    d = x_ref.shape[-1]
    j = pl.program_id(1)

    @pl.when(j == 0)
    def _():
        st_ref[...] = jnp.zeros_like(st_ref)
        carry_ref[...] = jnp.zeros_like(carry_ref)

    x = x_ref[...]
    mod = mod_ref[...]
    sh1, sc1, gt1 = mod[:, 0:d], mod[:, d:2 * d], mod[:, 2 * d:3 * d]
    h = (_rms_norm(x, g1_ref[...]) * (1.0 + sc1) + sh1).astype(BF16)

    seg = seg_ref[...]

    def rwkv_branch(r0, rows, delay):
        for _ in range(delay):
            yield
        p_rw = jnp.dot(h[r0:r0 + rows, :], w_in_ref[:, 0:RWKV_COLS], preferred_element_type=F32)
        yield
        row = lax.broadcasted_iota(jnp.int32, (rows, 1), 0)
        prev = jnp.where(row == 0, carry_ref[...], pltpu.roll(p_rw, 1, 0))
        carry_ref[...] = p_rw[rows - 1:rows, :]
        xs = p_rw + (prev - p_rw) * mu_ref[...]
        r = xs[:, 0:RWKV_WIDTH]
        k = xs[:, RWKV_WIDTH:2 * RWKV_WIDTH]
        v = xs[:, 2 * RWKV_WIDTH:3 * RWKV_WIDTH]
        lo = xs[:, COL_LORA:COL_GLORA]
        g_lo = xs[:, COL_GLORA:RWKV_COLS]
        yield
        lane = lax.broadcasted_iota(jnp.int32, (1, LANES), 1)
        lo_act = jnp.where(lane < DECAY_LORA, jnp.tanh(lo), lo)
        wa = _mm(lo_act, w2wa_ref[...])
        w_log = -jax.nn.softplus(-(w0_ref[...] + wa[:, 0:RWKV_WIDTH])) - 0.5
        a = jax.nn.sigmoid(a0_ref[...] + wa[:, RWKV_WIDTH:])
        gate = _mm(jax.nn.sigmoid(g_lo), w2g_ref[...])
        yield
        if has_vres:
            vmix = jax.nn.sigmoid(v0_ref[...] + _mm(_mm(v, w1r_ref[...]), w2r_ref[...]))
            v = v + (vfirst_ref[r0:r0 + rows, :] - v) * vmix
        else:
            vo_ref[r0:r0 + rows, :] = v
        kk = k * kk_ref[...]
        kk = kk * lax.rsqrt(jnp.maximum(_head_sums(kk * kk, seg), KK_NORM_FLOOR ** 2))
        k = k * (1.0 + (a - 1.0) * ka_ref[...])
        bonus = _head_sums(r * k * rk_ref[...], seg) * v
        yield
        y = yield from _scan_tile(r, -jnp.exp(w_log), k, v, -kk, kk * a, seg, st_ref)
        return y, bonus, gate

    rows = ts // ROW_RANGES
    streams = []
    for n in range(ROW_RANGES):
        streams.append(rwkv_branch(n * rows, rows, n * SCAN_LEAD))
        streams.append(_gmlp_and_gates(h[n * rows:(n + 1) * rows, :], w_in_ref, lnw_ref, lnb_ref,
                                       ws_ref, bst_ref, wbr_gm_ref, rows, n * SCAN_LEAD))
    outs = _interleave(*streams)
    y, bonus, gate = (jnp.concatenate(parts, axis=0) for parts in zip(*outs[0::2]))
    gate_rw, gm_term = (jnp.concatenate(parts, axis=0) for parts in zip(*outs[1::2]))
    mean = _head_sums(y, seg) * (1.0 / HEAD_DIM)
    yc = y - mean
    var = _head_sums(yc * yc, seg) * (1.0 / HEAD_DIM)
    y = yc * lax.rsqrt(var + GN_EPS) * gnw_ref[...] + gnb_ref[...]
    y_rw = (y + bonus) * gate

    merged = gate_rw * _mm(y_rw, wbr_rw_ref[...]) + gm_term
    xo_ref[...] = x + gt1 * _mm(merged, w_out_ref[...])


def _const_spec(shape, layer=None):
    if layer is None:
        return pl.BlockSpec(shape, lambda b, j: (0,) * len(shape), pipeline_mode=pl.Buffered(1))
    return pl.BlockSpec((None,) + shape, lambda b, j: (layer,) + (0,) * len(shape),
                        pipeline_mode=pl.Buffered(1))


def _token_mix(layer, x, mod, v_first, params, ts):
    batch, seq, d = x.shape
    has_vres = layer > 0
    tile = lambda w: pl.BlockSpec((None, ts, w), lambda b, j: (b, j, 0))
    row512 = _const_spec((1, RWKV_WIDTH), layer)
    p = params
    operands = [x, mod]
    in_specs = [tile(d), pl.BlockSpec((None, 1, N_MOD * d), lambda b, j: (b, 0, 0))]
    if has_vres:
        operands.append(v_first)
        in_specs.append(tile(RWKV_WIDTH))
    operands += [p["norm1_g"], p["w_in"], p["mu_shift"], p["w0_decay"], p["w2_wa"], p["a0"],
                 p["w2_gate"], p["k_k"], p["k_a"], p["r_k"], p["gn_w"], p["gn_b"]]
    in_specs += [_const_spec((1, d), layer), _const_spec(p["w_in"].shape[1:], layer),
                 _const_spec((1, RWKV_COLS), layer), row512,
                 _const_spec(p["w2_wa"].shape[1:], layer), row512,
                 _const_spec(p["w2_gate"].shape[1:], layer), row512, row512, row512, row512, row512]
    if has_vres:
        operands += [p["v0_res"], p["w1_res"], p["w2_res"]]
        in_specs += [_const_spec((1, RWKV_WIDTH), layer - 1),
                     _const_spec(p["w1_res"].shape[1:], layer - 1),
                     _const_spec(p["w2_res"].shape[1:], layer - 1)]
    operands += [p["ln_gmlp_w"], p["ln_gmlp_b"], p["w_spatial"], p["b_spatial_t"],
                 p["w_br_rwkv"], p["w_br_gmlp"], p["w_out"], p["seg_ones"]]
    in_specs += [_const_spec((1, GMLP_WIDTH), layer), _const_spec((1, GMLP_WIDTH), layer),
                 _const_spec(p["w_spatial"].shape[1:], layer),
                 _const_spec(p["b_spatial_t"].shape[1:], layer),
                 _const_spec(p["w_br_rwkv"].shape[1:], layer),
                 _const_spec(p["w_br_gmlp"].shape[1:], layer),
                 _const_spec(p["w_out"].shape[1:], layer),
                 _const_spec(p["seg_ones"].shape)]
    out_shape = [jax.ShapeDtypeStruct((batch, seq, d), F32)]
    out_specs = [tile(d)]
    if not has_vres:
        out_shape.append(jax.ShapeDtypeStruct((batch, seq, RWKV_WIDTH), F32))
        out_specs.append(tile(RWKV_WIDTH))
    scratch = [pltpu.VMEM((N_GROUPS, HEAD_DIM, MXU_WIDTH), F32), pltpu.VMEM((1, RWKV_COLS), F32)]
    outs = pl.pallas_call(
        functools.partial(_mix_kernel, has_vres, ts),
        grid=(batch, seq // ts),
        in_specs=in_specs,
        out_specs=out_specs,
        out_shape=out_shape,
        scratch_shapes=scratch,
        compiler_params=pltpu.CompilerParams(
            dimension_semantics=("arbitrary", "arbitrary"),
            vmem_limit_bytes=VMEM_LIMIT_BYTES),
    )(*operands)
    return outs if has_vres else (outs[0], outs[1])


def _ffn_kernel(final, ff_chunk, *refs):
    if final:
        x_ref, mod_ref, g2_ref, w1_ref, w2_ref, fg_ref, o_ref = refs
    else:
        x_ref, mod_ref, g2_ref, w1_ref, w2_ref, o_ref = refs
    d = x_ref.shape[-1]
    x = x_ref[...]
    mod = mod_ref[...]
    sh2, sc2, gt2 = mod[:, 3 * d:4 * d], mod[:, 4 * d:5 * d], mod[:, 5 * d:6 * d]
    h = (_rms_norm(x, g2_ref[...]) * (1.0 + sc2) + sh2).astype(BF16)
    acc = jnp.zeros_like(x)
    for c0 in range(0, w1_ref.shape[1], ff_chunk):
        t = jnp.maximum(jnp.dot(h, w1_ref[:, c0:c0 + ff_chunk], preferred_element_type=F32), 0.0)
        acc = acc + _mm(t * t, w2_ref[c0:c0 + ff_chunk, :])
    xn = x + gt2 * acc
    if final:
        xn = _rms_norm(xn, fg_ref[...])
    o_ref[...] = xn


def _channel_mix(layer, x, mod, params, final, tm, ff_chunk=FFN_CHUNK):
    batch, seq, d = x.shape
    tile = pl.BlockSpec((None, tm, d), lambda b, j: (b, j, 0))
    operands = [x, mod, params["norm2_g"], params["w_ff1"], params["w_ff2"]]
    in_specs = [tile, pl.BlockSpec((None, 1, N_MOD * d), lambda b, j: (b, 0, 0)),
                _const_spec((1, d), layer), _const_spec(params["w_ff1"].shape[1:], layer),
                _const_spec(params["w_ff2"].shape[1:], layer)]
    if final:
        operands.append(params["final_g"])
        in_specs.append(_const_spec((1, d)))
    return pl.pallas_call(
        functools.partial(_ffn_kernel, final, ff_chunk),
        grid=(batch, seq // tm),
        in_specs=in_specs,
        out_specs=tile,
        out_shape=jax.ShapeDtypeStruct((batch, seq, d), F32),
        compiler_params=pltpu.CompilerParams(
            dimension_semantics=("arbitrary", "arbitrary"),
            vmem_limit_bytes=VMEM_LIMIT_BYTES),
    )(*operands)


def kernel(x, c, w_ada, b_ada, norm1_g, norm2_g, w_in, mu_shift, w0_decay, w2_decay, a0, w2_aaa, w2_gate, k_k, k_a, r_k, gn_w, gn_b, v0_res, w1_res, w2_res, ln_gmlp_w, ln_gmlp_b, w_spatial, b_spatial, w_br_rwkv, w_br_gmlp, w_out, w_ff1, w_ff2, final_g):
    depth = w_in.shape[0]
    batch, seq, d = x.shape
    ts = min(MIX_TOKENS, seq)
    tm = min(FFN_TOKENS, seq)
    row = lambda t: t.reshape(t.shape[0], 1, -1)
    zeros = lambda *s: jnp.zeros(s, F32)
    w2_wa = jnp.concatenate([
        jnp.concatenate([w2_decay, zeros(depth, DECAY_LORA, RWKV_WIDTH)], axis=2),
        jnp.concatenate([zeros(depth, AAA_LORA, RWKV_WIDTH), w2_aaa], axis=2)], axis=1)
    pad = LANES - VRES_LORA
    head_id = jnp.arange(MXU_WIDTH) // HEAD_DIM
    params = dict(
        norm1_g=row(norm1_g), norm2_g=row(norm2_g), w_in=w_in.astype(BF16), mu_shift=row(mu_shift),
        w0_decay=row(w0_decay), w2_wa=w2_wa.astype(BF16), a0=row(a0), w2_gate=w2_gate.astype(BF16),
        k_k=row(k_k), k_a=row(k_a), r_k=row(r_k), gn_w=row(gn_w), gn_b=row(gn_b),
        v0_res=row(v0_res),
        w1_res=jnp.pad(w1_res, ((0, 0), (0, 0), (0, pad))).astype(BF16),
        w2_res=jnp.pad(w2_res, ((0, 0), (0, pad), (0, 0))).astype(BF16),
        ln_gmlp_w=row(ln_gmlp_w), ln_gmlp_b=row(ln_gmlp_b), w_spatial=w_spatial,
        b_spatial_t=jnp.swapaxes(b_spatial, 1, 2),
        w_br_rwkv=w_br_rwkv.astype(BF16), w_br_gmlp=w_br_gmlp.astype(BF16),
        w_out=w_out.astype(BF16), w_ff1=w_ff1.astype(BF16), w_ff2=w_ff2.astype(BF16),
        final_g=final_g.reshape(1, d),
        seg_ones=(head_id[:, None] == head_id[None, :]).astype(BF16),
    )
    mod = _modulation(c, w_ada, b_ada)
    v_first = None
    for layer in range(depth):
        mod_l = mod[layer].reshape(batch, 1, N_MOD * d)
        if layer == 0:
            x, v_first = _token_mix(layer, x, mod_l, None, params, ts)
        else:
            (x,) = _token_mix(layer, x, mod_l, v_first, params, ts)
        x = _channel_mix(layer, x, mod_l, params, layer == depth - 1, tm)
    return x
```
